```python
import math
import jax, jax.numpy as jnp
from jax import lax
import numpy as np

D_MODEL = 1024
BATCH = 16
SEQ = 2048
DEPTH = 4

A_HEADS = 8
A_HEAD_DIM = 64
A_WIDTH = A_HEADS * A_HEAD_DIM
IDX_HEADS = 4
IDX_DIM = 64
TOPK_MAX = 256
B_HEADS = 8
B_NOPE_DIM = 64
B_ROPE_DIM = 32
B_QK_DIM = B_NOPE_DIM + B_ROPE_DIM
B_V_DIM = 64
B_WIDTH = B_HEADS * B_V_DIM
Q_LORA = 256
KV_LORA = 128
ROPE_THETA = 10000.0
REL_BUCKETS = 32
REL_MAX_DIST = 128
N_BRANCHES = 2
D_FF = 4 * D_MODEL
Q_BLOCK = 128
EPS = 1e-6
NEG_INF = -1e30
IN_SIZES = (A_WIDTH, A_HEAD_DIM, A_HEAD_DIM, IDX_HEADS * IDX_DIM, IDX_DIM, IDX_HEADS,
            Q_LORA, KV_LORA, B_ROPE_DIM, N_BRANCHES * D_MODEL)
D_IN = sum(IN_SIZES)

kernel_name = "hybrid_dsa_mla_gated_trunk"


def rmsnorm(x, g):
    xf = x.astype(jnp.float32)
    y = xf * lax.rsqrt(jnp.mean(xf * xf, axis=-1, keepdims=True) + EPS)
    return (y * g.astype(jnp.float32)).astype(x.dtype)


def split_cols(z, sizes):
    cuts = [int(c) for c in np.cumsum(sizes)[:-1]]
    return jnp.split(z, cuts, axis=-1)


def rope_tables(positions):
    inv_freq = ROPE_THETA ** (-jnp.arange(0, B_ROPE_DIM, 2, dtype=jnp.float32) / B_ROPE_DIM)
    ang = positions.astype(jnp.float32)[..., None] * inv_freq
    return jnp.cos(ang), jnp.sin(ang)


def apply_rope(x, cos, sin):
    x1, x2 = jnp.split(x, 2, axis=-1)
    cos = cos.astype(x.dtype)
    sin = sin.astype(x.dtype)
    return jnp.concatenate([x1 * cos - x2 * sin, x1 * sin + x2 * cos], axis=-1)


def rel_bucket(dist):
    max_exact = REL_BUCKETS // 2
    n = jnp.maximum(dist, 0)
    nf = jnp.maximum(n.astype(jnp.float32), 1.0)
    log_b = max_exact + (jnp.log(nf / max_exact) / math.log(REL_MAX_DIST / max_exact)
                         * (REL_BUCKETS - max_exact)).astype(jnp.int32)
    return jnp.where(n < max_exact, n, jnp.minimum(log_b, REL_BUCKETS - 1))


def take_rows(arr, idx):
    return jax.vmap(lambda a, i: a[i])(arr, idx)


def dsa_sparse_attention(q, k, v, iq, ik, iw, positions, rel_table):
    bsz, L = q.shape[0], q.shape[1]
    k_top = min(TOPK_MAX, L // 4)
    key_idx = jnp.arange(L)
    ikf = ik.astype(jnp.float32)

    def one_block(bi):
        start = bi * Q_BLOCK
        qb = lax.dynamic_slice_in_dim(q, start, Q_BLOCK, axis=1)
        iqb = lax.dynamic_slice_in_dim(iq, start, Q_BLOCK, axis=1).astype(jnp.float32)
        iwb = lax.dynamic_slice_in_dim(iw, start, Q_BLOCK, axis=1).astype(jnp.float32) * IDX_HEADS ** -0.5
        posq = lax.dynamic_slice_in_dim(positions, start, Q_BLOCK, axis=1)
        q_idx = start + jnp.arange(Q_BLOCK)
        causal = key_idx[None, :] <= q_idx[:, None]
        dots = jnp.einsum('bqhd,bsd->bqhs', iqb, ikf) * IDX_DIM ** -0.5
        score = jnp.einsum('bqh,bqhs->bqs', iwb, jax.nn.relu(dots))
        score = jnp.where(causal[None], score, -jnp.inf)
        _, sel = lax.top_k(score, k_top)
        k_sel = take_rows(k, sel)
        v_sel = take_rows(v, sel)
        bias = rel_table[rel_bucket(posq[:, :, None] - take_rows(positions, sel))]
        logits = jnp.einsum('bqhd,bqkd->bqhk', qb, k_sel).astype(jnp.float32) * A_HEAD_DIM ** -0.5
        logits = logits + jnp.swapaxes(bias, 2, 3).astype(jnp.float32)
        valid = sel <= q_idx[None, :, None]
        logits = jnp.where(valid[:, :, None, :], logits, NEG_INF)
        p = jax.nn.softmax(logits, axis=-1).astype(v.dtype)
        return jnp.einsum('bqhk,bqkd->bqhd', p, v_sel)

    out = lax.map(one_block, jnp.arange(L // Q_BLOCK))
    return jnp.moveaxis(out, 0, 1).reshape(bsz, L, A_WIDTH)


def mla_attention(q_nope, q_rope, k_nope, k_rope, v):
    bsz, L = q_nope.shape[0], q_nope.shape[1]
    key_idx = jnp.arange(L)

    def one_block(bi):
        start = bi * Q_BLOCK
        qn = lax.dynamic_slice_in_dim(q_nope, start, Q_BLOCK, axis=1)
        qr = lax.dynamic_slice_in_dim(q_rope, start, Q_BLOCK, axis=1)
        q_idx = start + jnp.arange(Q_BLOCK)
        logits = (jnp.einsum('bqhd,bshd->bhqs', qn, k_nope)
                  + jnp.einsum('bqhr,bsr->bhqs', qr, k_rope)).astype(jnp.float32) * B_QK_DIM ** -0.5
        causal = key_idx[None, :] <= q_idx[:, None]
        logits = jnp.where(causal[None, None], logits, NEG_INF)
        p = jax.nn.softmax(logits, axis=-1).astype(v.dtype)
        return jnp.einsum('bhqs,bshd->bqhd', p, v)

    out = lax.map(one_block, jnp.arange(L // Q_BLOCK))
    return jnp.moveaxis(out, 0, 1).reshape(bsz, L, B_WIDTH)


def setup_inputs(seed: int = 0) -> dict:
    key = jax.random.key(seed)
    ks = jax.random.split(key, 20)
    f32 = jnp.float32

    def nrm(k, shape, scale):
        return jax.random.normal(k, shape, f32) * scale

    def gain(k, shape):
        return 1.0 + 0.02 * jax.random.normal(k, shape, f32)

    x = jax.random.normal(ks[0], (BATCH, SEQ, D_MODEL), f32)
    offset = jax.random.randint(ks[1], (BATCH, 1), 0, 4096, dtype=jnp.int32)
    positions = offset + jnp.arange(SEQ, dtype=jnp.int32)[None, :]
    return {
        "x": x,
        "positions": positions,
        "attn_norm_g": gain(ks[2], (DEPTH, D_MODEL)),
        "w_in": nrm(ks[3], (DEPTH, D_MODEL, D_IN), D_MODEL ** -0.5),
        "b_gate": nrm(ks[4], (DEPTH, N_BRANCHES * D_MODEL), 0.02),
        "q_latent_norm_g": gain(ks[5], (DEPTH, Q_LORA)),
        "kv_latent_norm_g": gain(ks[6], (DEPTH, KV_LORA)),
        "w_uq": nrm(ks[7], (DEPTH, Q_LORA, B_HEADS * B_QK_DIM), Q_LORA ** -0.5),
        "w_ukv": nrm(ks[8], (DEPTH, KV_LORA, B_HEADS * (B_NOPE_DIM + B_V_DIM)), KV_LORA ** -0.5),
        "w_branch_a": nrm(ks[9], (DEPTH, A_WIDTH, D_MODEL), A_WIDTH ** -0.5),
        "w_branch_b": nrm(ks[10], (DEPTH, B_WIDTH, D_MODEL), B_WIDTH ** -0.5),
        "w_out": nrm(ks[11], (DEPTH, D_MODEL, D_MODEL), D_MODEL ** -0.5),
        "mlp_norm_g": gain(ks[12], (DEPTH, D_MODEL)),
        "w_ff1": nrm(ks[13], (DEPTH, D_MODEL, D_FF), D_MODEL ** -0.5),
        "w_ff2": nrm(ks[14], (DEPTH, D_FF, D_MODEL), D_FF ** -0.5),
        "rel_bias": nrm(ks[15], (REL_BUCKETS, A_HEADS), 0.5),
        "final_norm_g": gain(ks[16], (D_MODEL,)),
    }


def reference(x, positions, attn_norm_g, w_in, b_gate, q_latent_norm_g, kv_latent_norm_g,
              w_uq, w_ukv, w_branch_a, w_branch_b, w_out, mlp_norm_g, w_ff1, w_ff2,
              rel_bias, final_norm_g):
    bsz, L, _ = x.shape
    cos, sin = rope_tables(positions)
    for l in range(DEPTH):
        h = rmsnorm(x, attn_norm_g[l])
        z = h @ w_in[l]
        a_q, a_k, a_v, i_q, i_k, i_w, c_q, c_kv, k_r, g = split_cols(z, IN_SIZES)
        y_a = dsa_sparse_attention(a_q.reshape(bsz, L, A_HEADS, A_HEAD_DIM), a_k, a_v,
                                   i_q.reshape(bsz, L, IDX_HEADS, IDX_DIM), i_k, i_w,
                                   positions, rel_bias)
        qb = (rmsnorm(c_q, q_latent_norm_g[l]) @ w_uq[l]).reshape(bsz, L, B_HEADS, B_QK_DIM)
        q_nope, q_rope = jnp.split(qb, [B_NOPE_DIM], axis=-1)
        q_rope = apply_rope(q_rope, cos[:, :, None], sin[:, :, None])
        kvb = (rmsnorm(c_kv, kv_latent_norm_g[l]) @ w_ukv[l]).reshape(bsz, L, B_HEADS, B_NOPE_DIM + B_V_DIM)
        k_nope, v_b = jnp.split(kvb, [B_NOPE_DIM], axis=-1)
        k_rope = apply_rope(k_r, cos, sin)
        y_b = mla_attention(q_nope, q_rope, k_nope, k_rope, v_b)
        gates = jax.nn.sigmoid((g + b_gate[l]).astype(jnp.float32)).astype(x.dtype)
        gates = gates.reshape(bsz, L, N_BRANCHES, D_MODEL)
        merged = gates[:, :, 0] * (y_a @ w_branch_a[l]) + gates[:, :, 1] * (y_b @ w_branch_b[l])
        x = x + merged @ w_out[l]
        h2 = rmsnorm(x, mlp_norm_g[l])
        x = x + jnp.square(jax.nn.relu(h2 @ w_ff1[l])) @ w_ff2[l]
    return rmsnorm(x, final_norm_g)
```

```python
import functools
import math

import jax
import jax.numpy as jnp
from jax import lax
from jax.experimental import pallas as pl
from jax.experimental.pallas import tpu as pltpu

F32 = jnp.float32
BF16 = jnp.bfloat16
I32 = jnp.int32

D_MODEL = 1024
A_HEADS = 8
A_HEAD_DIM = 64
A_WIDTH = A_HEADS * A_HEAD_DIM
IDX_HEADS = 4
IDX_DIM = 64
TOPK_MAX = 256
B_HEADS = 8
B_NOPE_DIM = 64
B_ROPE_DIM = 32
B_QK_DIM = B_NOPE_DIM + B_ROPE_DIM
B_V_DIM = 64
B_WIDTH = B_HEADS * B_V_DIM
Q_LORA = 256
KV_LORA = 128
ROPE_THETA = 10000.0
REL_BUCKETS = 32
REL_MAX_DIST = 128
N_BRANCHES = 2
D_FF = 4 * D_MODEL
EPS = 1e-6
NEG_INF = -1e30
IN_SIZES = (A_WIDTH, A_HEAD_DIM, A_HEAD_DIM, IDX_HEADS * IDX_DIM, IDX_DIM, IDX_HEADS,
            Q_LORA, KV_LORA, B_ROPE_DIM, N_BRANCHES * D_MODEL)

LANES = 128
QB = 128
KC = 2 * LANES
INT_MIN = -2 ** 31
G_IK0, G_IK1, G_KK0, G_KK1, G_VA, G_VB = range(6)
W2_COLS = 9 * LANES
VMEM_LIMIT = 56 * 1024 * 1024

NT_DIMS = (((1,), (1,)), ((), ()))


def _rms(x, g):
    return x * lax.rsqrt(jnp.mean(x * x, axis=-1, keepdims=True) + EPS) * g


def _proj_kernel(x_ref, g_ref, w1_ref, w2_ref, gq_ref, gkv_ref, wuq_ref, wukv_ref, rope_ref,
                 aq_ref, iq_ref, kvi_ref, iw_ref, qm_ref, km_ref, vm_ref):
    tm = x_ref.shape[0]
    hb = _rms(x_ref[...], g_ref[...]).astype(BF16)
    z1 = jnp.dot(hb, w1_ref[...], preferred_element_type=F32)
    z2 = jnp.dot(hb, w2_ref[...], preferred_element_type=F32)
    aq_ref[...] = z1[:, :A_WIDTH].astype(BF16)
    iq_ref[...] = z1[:, A_WIDTH:A_WIDTH + IDX_HEADS * IDX_DIM].astype(BF16)
    lane = lax.broadcasted_iota(I32, (tm, LANES), 1)
    for gi in range(6):
        blk = z2[:, gi * LANES:(gi + 1) * LANES]
        if gi == G_VA:
            blk = jnp.where(lane == A_HEAD_DIM, 1.0, blk)
        if gi == G_VB:
            blk = jnp.where(lane == 0, 1.0, blk)
        kvi_ref[gi] = blk.astype(BF16)
    iw_ref[...] = z2[:, 8 * LANES:9 * LANES]

    cs, s1, s2 = rope_ref[0], rope_ref[1], rope_ref[2]

    def rope(u):
        return u * cs + pltpu.roll(u, 16, 1) * s1 + pltpu.roll(u, LANES - 16, 1) * s2

    cq = z1[:, A_WIDTH + IDX_HEADS * IDX_DIM:]
    qup = jnp.dot(_rms(cq, gq_ref[...]).astype(BF16), wuq_ref[...], preferred_element_type=F32)
    scale = B_QK_DIM ** -0.5
    for h in range(B_HEADS):
        qm_ref[0, h] = (rope(qup[:, h * LANES:(h + 1) * LANES]) * scale).astype(BF16)
    ckv = z2[:, 6 * LANES:7 * LANES]
    kvup = jnp.dot(_rms(ckv, gkv_ref[...]).astype(BF16), wukv_ref[...],
                   preferred_element_type=F32)
    kr = rope(z2[:, 7 * LANES:8 * LANES])
    for h in range(B_HEADS):
        km_ref[0, h] = (kvup[:, h * LANES:(h + 1) * LANES] + kr).astype(BF16)
    for p in range(B_HEADS // 2):
        c0 = B_HEADS * LANES + p * LANES
        vm_ref[0, p] = kvup[:, c0:c0 + LANES].astype(BF16)


def _proj(x2d, g, w1, w2, gq, gkv, wuq, wukv, rope, bsz, seq, tm):
    t = x2d.shape[0]
    nt = seq // tm
    const = lambda i: (0, 0)
    return pl.pallas_call(
        _proj_kernel,
        grid=(t // tm,),
        in_specs=[
            pl.BlockSpec((tm, D_MODEL), lambda i: (i, 0)),
            pl.BlockSpec((1, D_MODEL), const),
            pl.BlockSpec(w1.shape, const),
            pl.BlockSpec(w2.shape, const),
            pl.BlockSpec((1, Q_LORA), const),
            pl.BlockSpec((1, KV_LORA), const),
            pl.BlockSpec(wuq.shape, const),
            pl.BlockSpec(wukv.shape, const),
            pl.BlockSpec((3, tm, LANES), lambda i: (0, i, 0)),
        ],
        out_specs=[
            pl.BlockSpec((tm, A_WIDTH), lambda i: (i, 0)),
            pl.BlockSpec((tm, IDX_HEADS * IDX_DIM), lambda i: (i, 0)),
            pl.BlockSpec((6, tm, LANES), lambda i: (0, i, 0)),
            pl.BlockSpec((tm, LANES), lambda i: (i, 0)),
            pl.BlockSpec((1, B_HEADS, tm, LANES), lambda i: (i // nt, 0, i % nt, 0)),
            pl.BlockSpec((1, B_HEADS, tm, LANES), lambda i: (i // nt, 0, i % nt, 0)),
            pl.BlockSpec((1, B_HEADS // 2, tm, LANES), lambda i: (i // nt, 0, i % nt, 0)),
        ],
        out_shape=[
            jax.ShapeDtypeStruct((t, A_WIDTH), BF16),
            jax.ShapeDtypeStruct((t, IDX_HEADS * IDX_DIM), BF16),
            jax.ShapeDtypeStruct((6, t, LANES), BF16),
            jax.ShapeDtypeStruct((t, LANES), F32),
            jax.ShapeDtypeStruct((bsz, B_HEADS, seq, LANES), BF16),
            jax.ShapeDtypeStruct((bsz, B_HEADS, seq, LANES), BF16),
            jax.ShapeDtypeStruct((bsz, B_HEADS // 2, seq, LANES), BF16),
        ],
        compiler_params=pltpu.CompilerParams(
            dimension_semantics=("arbitrary",), vmem_limit_bytes=VMEM_LIMIT),
        name="proj",
    )(x2d, g, w1, w2, gq, gkv, wuq, wukv, rope)


def _dsa_kernel(aq_ref, iq_ref, iw_ref, kvi_ref, bias_ref, out_ref,
                key_ref, m_ref, lg_ref, mx_ref, acc_ref, *, k_top):
    i = pl.program_id(1)
    nk = i + 1
    n2 = (i + 2) // 2
    par = i % 2
    kf = float(k_top)

    row = lax.broadcasted_iota(I32, (QB, LANES), 0)
    col = lax.broadcasted_iota(I32, (QB, LANES), 1)

    iq = iq_ref[...]
    iqs = jnp.concatenate([iq[:, :LANES], iq[:, LANES:]], axis=0)
    iw_t = iw_ref[...].T
    w0, w1, w2, w3 = (iw_t[h:h + 1, :] for h in range(IDX_HEADS))

    def score_body(c, carry):
        start = pl.multiple_of(c * QB, QB)
        t0 = lax.dot_general(kvi_ref[G_IK0, pl.ds(start, QB), :], iqs, NT_DIMS,
                             preferred_element_type=F32)
        t1 = lax.dot_general(kvi_ref[G_IK1, pl.ds(start, QB), :], iqs, NT_DIMS,
                             preferred_element_type=F32)
        s = (w0 * jnp.maximum(t0[:, :LANES], 0.0) + w1 * jnp.maximum(t1[:, :LANES], 0.0)
             + w2 * jnp.maximum(t0[:, LANES:], 0.0) + w3 * jnp.maximum(t1[:, LANES:], 0.0))
        b = lax.bitcast_convert_type(s, I32)
        sk = jnp.where(b < 0, b ^ 0x7FFFFFFF, b)
        sk = jnp.where(c * QB + row > i * QB + col, INT_MIN, sk)
        key_ref[pl.ds(start, QB), :] = sk
        return carry

    lax.fori_loop(0, nk, score_body, 0)

    def count(cand, strict):
        cb = jnp.broadcast_to(cand, (8, LANES))

        def body(c, accs):
            start = pl.multiple_of(c * QB, QB)
            accs = list(accs)
            for r in range(QB // 8):
                blk = key_ref[pl.ds(start + 8 * r, 8), :]
                hit = (blk > cb) if strict else (blk >= cb)
                accs[r % 4] = accs[r % 4] + jnp.where(hit, 1.0, 0.0)
            return tuple(accs)

        z = jnp.zeros((8, LANES), F32)
        a = lax.fori_loop(0, nk, body, (z, z, z, z))
        return jnp.sum((a[0] + a[1]) + (a[2] + a[3]), axis=0, keepdims=True)

    c_nonneg = count(jnp.zeros((1, LANES), I32), False)
    base = jnp.where(c_nonneg >= kf, 0, INT_MIN).astype(I32)

    def bit_body(it, base):
        cand = base + lax.shift_left(jnp.int32(1), 30 - it)
        return jnp.where(count(cand, False) >= kf, cand, base)

    tau = lax.fori_loop(0, 31, bit_body, base)
    need = kf - count(tau, True)

    ltri = jnp.where(row >= col, 1.0, 0.0).astype(BF16)

    def mask_body(c, carry):
        blk = key_ref[pl.ds(pl.multiple_of(c * QB, QB), QB), :]
        eq = blk == tau
        pref = jnp.dot(ltri, jnp.where(eq, 1.0, 0.0).astype(BF16),
                       preferred_element_type=F32)
        sel = (blk > tau) | (eq & (pref + carry <= need))
        sel = sel & (c * QB + row <= i * QB + col)
        m_ref[c] = jnp.where(sel, jnp.inf, NEG_INF).T
        return carry + pref[QB - 1:QB, :]

    lax.fori_loop(0, nk, mask_body, jnp.zeros((1, LANES), F32))

    @pl.when(par == 0)
    def _():
        m_ref[i + 1] = jnp.full((QB, LANES), NEG_INF, F32)

    aq = aq_ref[...]
    qs = jnp.concatenate([aq[:, j * LANES:(j + 1) * LANES] for j in range(A_HEADS // 2)],
                         axis=0)
    mx_ref[...] = jnp.full(mx_ref.shape, -jnp.inf, F32)
    nrow = (A_HEADS // 2) * QB

    def logits_chunk(c2, bias_pos):
        start = pl.multiple_of(c2 * KC, KC)
        mc = jnp.concatenate([m_ref[2 * c2], m_ref[2 * c2 + 1]], axis=1)
        for r, g in ((0, G_KK0), (1, G_KK1)):
            lg = lax.dot_general(qs, kvi_ref[g, pl.ds(start, KC), :], NT_DIMS,
                                 preferred_element_type=F32)
            if bias_pos is not None:
                lg = lg + bias_ref[par, bias_pos, r]
            for j in range(A_HEADS // 2):
                rows = slice(j * QB, (j + 1) * QB)
                lj = jnp.minimum(lg[rows], mc)
                lg_ref[r, c2, rows, :] = lj
                mrows = slice(r * nrow + j * QB, r * nrow + (j + 1) * QB)
                mx_ref[mrows, :] = jnp.maximum(mx_ref[mrows, :],
                                               jnp.maximum(lj[:, :LANES], lj[:, LANES:]))

    def far_body(c2, carry):
        logits_chunk(c2, None)
        return carry

    lax.fori_loop(0, n2 - 2, far_body, 0)

    @pl.when(n2 >= 2)
    def _():
        logits_chunk(n2 - 2, 0)

    logits_chunk(n2 - 1, 1)

    mrow = jnp.max(mx_ref[...], axis=1, keepdims=True)
    acc_ref[...] = jnp.zeros(acc_ref.shape, F32)

    def pv_body(c2, carry):
        start = pl.multiple_of(c2 * KC, KC)
        for r, g in ((0, G_VA), (1, G_VB)):
            p = jnp.exp(lg_ref[r, c2] - mrow[r * nrow:(r + 1) * nrow])
            acc_ref[r * nrow:(r + 1) * nrow, :] += jnp.dot(
                p.astype(BF16), kvi_ref[g, pl.ds(start, KC), :], preferred_element_type=F32)
        return carry

    lax.fori_loop(0, n2, pv_body, 0)

    for j in range(A_HEADS // 2):
        oe = acc_ref[j * QB:(j + 1) * QB, :]
        oo = acc_ref[nrow + j * QB:nrow + (j + 1) * QB, :]
        out = jnp.where(col < A_HEAD_DIM, oe / oe[:, A_HEAD_DIM:A_HEAD_DIM + 1], oo / oo[:, 0:1])
        out_ref[:, j * LANES:(j + 1) * LANES] = out.astype(BF16)


def _dsa(aq, iq, iw, kvi, bias, bsz, seq, k_top):
    t = aq.shape[0]
    nq = seq // QB
    assert nq % 2 == 0
    return pl.pallas_call(
        functools.partial(_dsa_kernel, k_top=k_top),
        grid=(bsz, nq),
        in_specs=[
            pl.BlockSpec((QB, A_WIDTH), lambda b, i: (b * nq + i, 0)),
            pl.BlockSpec((QB, IDX_HEADS * IDX_DIM), lambda b, i: (b * nq + i, 0)),
            pl.BlockSpec((QB, LANES), lambda b, i: (b * nq + i, 0)),
            pl.BlockSpec((6, seq, LANES), lambda b, i: (0, b, 0)),
            pl.BlockSpec(bias.shape, lambda b, i: (0, 0, 0, 0, 0)),
        ],
        out_specs=pl.BlockSpec((QB, A_WIDTH), lambda b, i: (b * nq + i, 0)),
        out_shape=jax.ShapeDtypeStruct((t, A_WIDTH), BF16),
        scratch_shapes=[
            pltpu.VMEM((seq, LANES), I32),
            pltpu.VMEM((nq, QB, LANES), F32),
            pltpu.VMEM((2, seq // KC, (A_HEADS // 2) * QB, KC), F32),
            pltpu.VMEM((A_HEADS * QB, LANES), F32),
            pltpu.VMEM((A_HEADS * QB, LANES), F32),
        ],
        compiler_params=pltpu.CompilerParams(
            dimension_semantics=("arbitrary", "arbitrary"), vmem_limit_bytes=VMEM_LIMIT),
        name="dsa",
    )(aq, iq, iw, kvi, bias)


def _mla_kernel(q_ref, k_ref, v_ref, out_ref, lg_ref, mx_ref, ls_ref, acc_ref):
    tq = q_ref.shape[2]
    i = pl.program_id(2)
    row = lax.broadcasted_iota(I32, (tq, tq), 0)
    col = lax.broadcasted_iota(I32, (tq, tq), 1)
    lane = lax.broadcasted_iota(I32, (tq, LANES), 1)
    outs = []
    for e in range(2):
        q = q_ref[0, e]
        mx_ref[...] = jnp.full(mx_ref.shape, -jnp.inf, F32)

        def logits(c, diag):
            k = k_ref[0, e, pl.ds(pl.multiple_of(c * tq, tq), tq), :]
            lg = lax.dot_general(q, k, NT_DIMS, preferred_element_type=F32)
            if diag:
                lg = jnp.where(col <= row, lg, NEG_INF)
            lg_ref[c] = lg
            part = lg[:, :LANES]
            for s in range(1, tq // LANES):
                part = jnp.maximum(part, lg[:, s * LANES:(s + 1) * LANES])
            mx_ref[...] = jnp.maximum(mx_ref[...], part)

        def p1(c, carry):
            logits(c, False)
            return carry

        lax.fori_loop(0, i, p1, 0)
        logits(i, True)
        m = jnp.max(mx_ref[...], axis=1, keepdims=True)
        ls_ref[...] = jnp.zeros(ls_ref.shape, F32)
        acc_ref[...] = jnp.zeros(acc_ref.shape, F32)

        def p2(c, carry):
            p = jnp.exp(lg_ref[c] - m)
            part = p[:, :LANES]
            for s in range(1, tq // LANES):
                part = part + p[:, s * LANES:(s + 1) * LANES]
            ls_ref[...] += part
            v = v_ref[0, 0, pl.ds(pl.multiple_of(c * tq, tq), tq), :]
            acc_ref[...] += jnp.dot(p.astype(BF16), v, preferred_element_type=F32)
            return carry

        lax.fori_loop(0, i + 1, p2, 0)
        outs.append(acc_ref[...] / jnp.sum(ls_ref[...], axis=1, keepdims=True))
    out_ref[0] = jnp.where(lane < B_V_DIM, outs[0], outs[1]).astype(BF16)


def _mla(qm, km, vm, tq):
    bsz, _, seq, _ = qm.shape
    return pl.pallas_call(
        _mla_kernel,
        grid=(bsz, B_HEADS // 2, seq // tq),
        in_specs=[
            pl.BlockSpec((1, 2, tq, LANES), lambda b, p, i: (b, p, i, 0)),
            pl.BlockSpec((1, 2, seq, LANES), lambda b, p, i: (b, p, 0, 0)),
            pl.BlockSpec((1, 1, seq, LANES), lambda b, p, i: (b, p, 0, 0)),
        ],
        out_specs=pl.BlockSpec((1, tq, LANES), lambda b, p, i: (b, i, p)),
        out_shape=jax.ShapeDtypeStruct((bsz, seq, B_WIDTH), BF16),
        scratch_shapes=[
            pltpu.VMEM((seq // tq, tq, tq), F32),
            pltpu.VMEM((tq, LANES), F32),
            pltpu.VMEM((tq, LANES), F32),
            pltpu.VMEM((tq, LANES), F32),
        ],
        compiler_params=pltpu.CompilerParams(
            dimension_semantics=("arbitrary", "arbitrary", "arbitrary"),
            vmem_limit_bytes=VMEM_LIMIT),
        name="mla",
    )(qm, km, vm)


def _merge_kernel(x_ref, g_ref, wg_ref, bg_ref, ya_ref, yb_ref, wa_ref, wb_ref, wo_ref, o_ref):
    x = x_ref[...]
    hb = _rms(x, g_ref[...]).astype(BF16)
    gl = jnp.dot(hb, wg_ref[...], preferred_element_type=F32) + bg_ref[...]
    gates = 1.0 / (1.0 + jnp.exp(-gl))
    pa = jnp.dot(ya_ref[...], wa_ref[...], preferred_element_type=F32)
    pb = jnp.dot(yb_ref[...], wb_ref[...], preferred_element_type=F32)
    merged = gates[:, :D_MODEL] * pa + gates[:, D_MODEL:] * pb
    o_ref[...] = x + jnp.dot(merged.astype(BF16), wo_ref[...], preferred_element_type=F32)


def _merge(x2d, g, wg, bg, ya, yb, wa, wb, wo, tm):
    t = x2d.shape[0]
    const = lambda i: (0, 0)
    return pl.pallas_call(
        _merge_kernel,
        grid=(t // tm,),
        in_specs=[
            pl.BlockSpec((tm, D_MODEL), lambda i: (i, 0)),
            pl.BlockSpec((1, D_MODEL), const),
            pl.BlockSpec(wg.shape, const),
            pl.BlockSpec((1, N_BRANCHES * D_MODEL), const),
            pl.BlockSpec((tm, A_WIDTH), lambda i: (i, 0)),
            pl.BlockSpec((tm, B_WIDTH), lambda i: (i, 0)),
            pl.BlockSpec(wa.shape, const),
            pl.BlockSpec(wb.shape, const),
            pl.BlockSpec(wo.shape, const),
        ],
        out_specs=pl.BlockSpec((tm, D_MODEL), lambda i: (i, 0)),
        out_shape=jax.ShapeDtypeStruct((t, D_MODEL), F32),
        compiler_params=pltpu.CompilerParams(
            dimension_semantics=("arbitrary",), vmem_limit_bytes=VMEM_LIMIT),
        name="merge",
    )(x2d, g, wg, bg, ya, yb, wa, wb, wo)


def _mlp_kernel(x_ref, g_ref, w1_ref, w2_ref, gf_ref, o_ref, *, final, fc):
    x = x_ref[...]
    hb = _rms(x, g_ref[...]).astype(BF16)
    acc = jnp.zeros(x.shape, F32)
    for j in range(D_FF // fc):
        a = jnp.dot(hb, w1_ref[:, j * fc:(j + 1) * fc], preferred_element_type=F32)
        a = jnp.square(jnp.maximum(a, 0.0)).astype(BF16)
        acc = acc + jnp.dot(a, w2_ref[j * fc:(j + 1) * fc, :], preferred_element_type=F32)
    y = x + acc
    if final:
        y = _rms(y, gf_ref[...])
    o_ref[...] = y


def _mlp(x2d, g, w1, w2, gf, tm, final):
    t = x2d.shape[0]
    const = lambda i: (0, 0)
    return pl.pallas_call(
        functools.partial(_mlp_kernel, final=final, fc=1024),
        grid=(t // tm,),
        in_specs=[
            pl.BlockSpec((tm, D_MODEL), lambda i: (i, 0)),
            pl.BlockSpec((1, D_MODEL), const),
            pl.BlockSpec(w1.shape, const, pipeline_mode=pl.Buffered(1)),
            pl.BlockSpec(w2.shape, const, pipeline_mode=pl.Buffered(1)),
            pl.BlockSpec((1, D_MODEL), const),
        ],
        out_specs=pl.BlockSpec((tm, D_MODEL), lambda i: (i, 0)),
        out_shape=jax.ShapeDtypeStruct((t, D_MODEL), F32),
        compiler_params=pltpu.CompilerParams(
            dimension_semantics=("arbitrary",), vmem_limit_bytes=VMEM_LIMIT),
        name="mlp",
    )(x2d, g, w1, w2, gf)


def _split_w_in(w_in):
    cuts = [0]
    for s in IN_SIZES:
        cuts.append(cuts[-1] + s)
    return [w_in[..., cuts[n]:cuts[n + 1]] for n in range(len(IN_SIZES))]


def _prep_weights(w_in, w_uq, w_ukv):
    depth = w_in.shape[0]
    a_q, a_k, a_v, i_q, i_k, i_w, c_q, c_kv, k_r, w_g = _split_w_in(w_in)
    z64 = jnp.zeros((depth, D_MODEL, 64), F32)
    w1 = jnp.concatenate([a_q * A_HEAD_DIM ** -0.5, i_q * IDX_DIM ** -0.5, c_q], axis=-1)
    w2 = jnp.concatenate([
        i_k, z64, z64, i_k, a_k, z64, z64, a_k, a_v, z64, z64, a_v, c_kv,
        z64, k_r, jnp.zeros((depth, D_MODEL, 32), F32),
        i_w * IDX_HEADS ** -0.5, jnp.zeros((depth, D_MODEL, LANES - IDX_HEADS), F32)], axis=-1)
    assert w2.shape[-1] == W2_COLS
    wuq = w_uq.reshape(depth, Q_LORA, B_HEADS, B_QK_DIM)
    wuq = jnp.pad(wuq, ((0, 0), (0, 0), (0, 0), (0, LANES - B_QK_DIM)))
    wuq = wuq.reshape(depth, Q_LORA, B_HEADS * LANES)
    wukv = w_ukv.reshape(depth, KV_LORA, B_HEADS, B_NOPE_DIM + B_V_DIM)
    wk = jnp.pad(wukv[..., :B_NOPE_DIM], ((0, 0), (0, 0), (0, 0), (0, LANES - B_NOPE_DIM)))
    wk = wk.reshape(depth, KV_LORA, B_HEADS * LANES)
    wv = wukv[..., B_NOPE_DIM:].reshape(depth, KV_LORA, B_HEADS * B_V_DIM)
    wukv = jnp.concatenate([wk, wv], axis=-1)
    return (w1.astype(BF16), w2.astype(BF16), w_g.astype(BF16), wuq.astype(BF16),
            wukv.astype(BF16))


def _rope_tables(positions):
    inv_freq = ROPE_THETA ** (-jnp.arange(0, B_ROPE_DIM, 2, dtype=F32) / B_ROPE_DIM)
    ang = positions.astype(F32).reshape(-1)[:, None] * inv_freq
    cos, sin = jnp.cos(ang), jnp.sin(ang)
    t = ang.shape[0]
    half = B_ROPE_DIM // 2
    one = jnp.ones((t, B_NOPE_DIM), F32)
    z = lambda n: jnp.zeros((t, n), F32)
    cs = jnp.concatenate([one, cos, cos, z(LANES - B_QK_DIM)], axis=1)
    s1 = jnp.concatenate([z(B_NOPE_DIM + half), sin, z(LANES - B_QK_DIM)], axis=1)
    s2 = jnp.concatenate([z(B_NOPE_DIM), -sin, z(half + LANES - B_QK_DIM)], axis=1)
    return jnp.stack([cs, s1, s2])


def _rel_bucket(dist):
    max_exact = REL_BUCKETS // 2
    n = jnp.maximum(dist, 0)
    nf = jnp.maximum(n.astype(F32), 1.0)
    log_b = max_exact + (jnp.log(nf / max_exact) / math.log(REL_MAX_DIST / max_exact)
                         * (REL_BUCKETS - max_exact)).astype(I32)
    return jnp.where(n < max_exact, n, jnp.minimum(log_b, REL_BUCKETS - 1))


def _bias_tables(rel_bias):
    d = jnp.arange(0, 2 * QB, dtype=I32)
    tab = rel_bias[_rel_bucket(d)] - rel_bias[REL_BUCKETS - 1][None, :]
    q = jnp.arange(QB)[:, None]
    s = jnp.arange(QB)[None, :]
    diag = jnp.where((q >= s)[..., None], tab[jnp.clip(q - s, 0, 2 * QB - 1)], 0.0)
    prev = tab[QB + q - s]
    zero = jnp.zeros_like(diag)
    cat = lambda a, b: jnp.concatenate([a, b], axis=1)
    table = jnp.stack([jnp.stack([cat(zero, prev), cat(diag, zero)]),
                       jnp.stack([cat(zero, zero), cat(prev, diag)])])
    table = table.reshape(2, 2, QB, KC, A_HEADS // 2, 2)
    table = jnp.transpose(table, (0, 1, 5, 4, 2, 3))
    return table.reshape(2, 2, 2, (A_HEADS // 2) * QB, KC).astype(F32)


def kernel(x, positions, attn_norm_g, w_in, b_gate, q_latent_norm_g, kv_latent_norm_g, w_uq,
           w_ukv, w_branch_a, w_branch_b, w_out, mlp_norm_g, w_ff1, w_ff2, rel_bias,
           final_norm_g):
    bsz, seq, _ = x.shape
    depth = w_in.shape[0]
    t = bsz * seq
    k_top = min(TOPK_MAX, seq // 4)
    tm = min(512, seq)
    tq = min(256, seq)

    w1, w2, wg, wuq, wukv = _prep_weights(w_in, w_uq, w_ukv)
    wa, wb, wo = w_branch_a.astype(BF16), w_branch_b.astype(BF16), w_out.astype(BF16)
    wf1, wf2 = w_ff1.astype(BF16), w_ff2.astype(BF16)
    rope = _rope_tables(positions)
    bias = _bias_tables(rel_bias)
    gf = final_norm_g.reshape(1, D_MODEL)

    x2d = x.reshape(t, D_MODEL)
    for l in range(depth):
        g_attn = attn_norm_g[l].reshape(1, D_MODEL)
        aq, iq, kvi, iw, qm, km, vm = _proj(
            x2d, g_attn, w1[l], w2[l], q_latent_norm_g[l].reshape(1, Q_LORA),
            kv_latent_norm_g[l].reshape(1, KV_LORA), wuq[l], wukv[l], rope, bsz, seq, tm)
        ya = _dsa(aq, iq, iw, kvi, bias, bsz, seq, k_top)
        yb = _mla(qm, km, vm, tq).reshape(t, B_WIDTH)
        x2d = _merge(x2d, g_attn, wg[l], b_gate[l].reshape(1, N_BRANCHES * D_MODEL), ya, yb,
                     wa[l], wb[l], wo[l], tm)
        x2d = _mlp(x2d, mlp_norm_g[l].reshape(1, D_MODEL), wf1[l], wf2[l], gf, tm,
                   final=(l == depth - 1))
    return x2d.reshape(bsz, seq, D_MODEL)
```

```python
import functools
import math

import jax
import jax.numpy as jnp
from jax import lax
from jax.experimental import pallas as pl
from jax.experimental.pallas import tpu as pltpu

F32 = jnp.float32
BF16 = jnp.bfloat16
I32 = jnp.int32

D_MODEL = 1024
A_HEADS = 8
A_HEAD_DIM = 64
A_WIDTH = A_HEADS * A_HEAD_DIM
IDX_HEADS = 4
IDX_DIM = 64
TOPK_MAX = 256
B_HEADS = 8
B_NOPE_DIM = 64
B_ROPE_DIM = 32
B_QK_DIM = B_NOPE_DIM + B_ROPE_DIM
B_V_DIM = 64
B_WIDTH = B_HEADS * B_V_DIM
Q_LORA = 256
KV_LORA = 128
ROPE_THETA = 10000.0
REL_BUCKETS = 32
REL_MAX_DIST = 128
N_BRANCHES = 2
D_FF = 4 * D_MODEL
EPS = 1e-6
NEG_INF = -1e30
IN_SIZES = (A_WIDTH, A_HEAD_DIM, A_HEAD_DIM, IDX_HEADS * IDX_DIM, IDX_DIM, IDX_HEADS,
            Q_LORA, KV_LORA, B_ROPE_DIM, N_BRANCHES * D_MODEL)

LANES = 128
SUBLANES = 8
QB = 2 * LANES
KC = QB
NPAIR = A_HEADS // 2
INT_MIN = -2 ** 31
G_IK0, G_IK1, G_KK0, G_KK1, G_VA, G_VB = range(6)
W2_COLS = 9 * LANES
VMEM_LIMIT = 56 * 1024 * 1024

NT_DIMS = (((1,), (1,)), ((), ()))


def _rms(x, g):
    return x * lax.rsqrt(jnp.mean(x * x, axis=-1, keepdims=True) + EPS) * g


def _proj_kernel(x_ref, g_ref, w1_ref, w2_ref, gq_ref, gkv_ref, wuq_ref, wukv_ref, rope_ref,
                 aq_ref, iq_ref, kvi_ref, iw_ref, qm_ref, km_ref, vm_ref):
    tm = x_ref.shape[0]
    hb = _rms(x_ref[...], g_ref[...]).astype(BF16)
    z1 = jnp.dot(hb, w1_ref[...], preferred_element_type=F32)
    z2 = jnp.dot(hb, w2_ref[...], preferred_element_type=F32)
    aq_ref[...] = z1[:, :A_WIDTH].astype(BF16)
    iq_ref[...] = z1[:, A_WIDTH:A_WIDTH + IDX_HEADS * IDX_DIM].astype(BF16)
    lane = lax.broadcasted_iota(I32, (tm, LANES), 1)
    for gi in range(6):
        blk = z2[:, gi * LANES:(gi + 1) * LANES]
        if gi == G_VA:
            blk = jnp.where(lane == A_HEAD_DIM, 1.0, blk)
        if gi == G_VB:
            blk = jnp.where(lane == 0, 1.0, blk)
        kvi_ref[gi] = blk.astype(BF16)
    iw_ref[...] = z2[:, 8 * LANES:9 * LANES]

    cs, s1, s2 = rope_ref[0], rope_ref[1], rope_ref[2]

    def rope(u):
        return u * cs + pltpu.roll(u, 16, 1) * s1 + pltpu.roll(u, LANES - 16, 1) * s2

    cq = z1[:, A_WIDTH + IDX_HEADS * IDX_DIM:]
    qup = jnp.dot(_rms(cq, gq_ref[...]).astype(BF16), wuq_ref[...], preferred_element_type=F32)
    scale = B_QK_DIM ** -0.5
    for h in range(B_HEADS):
        qm_ref[0, h] = (rope(qup[:, h * LANES:(h + 1) * LANES]) * scale).astype(BF16)
    ckv = z2[:, 6 * LANES:7 * LANES]
    kvup = jnp.dot(_rms(ckv, gkv_ref[...]).astype(BF16), wukv_ref[...],
                   preferred_element_type=F32)
    kr = rope(z2[:, 7 * LANES:8 * LANES])
    for h in range(B_HEADS):
        km_ref[0, h] = (kvup[:, h * LANES:(h + 1) * LANES] + kr).astype(BF16)
    for p in range(B_HEADS // 2):
        c0 = B_HEADS * LANES + p * LANES
        vm_ref[0, p] = kvup[:, c0:c0 + LANES].astype(BF16)


def _proj(x2d, g, w1, w2, gq, gkv, wuq, wukv, rope, bsz, seq, tm):
    t = x2d.shape[0]
    nt = seq // tm
    const = lambda i: (0, 0)
    return pl.pallas_call(
        _proj_kernel,
        grid=(t // tm,),
        in_specs=[
            pl.BlockSpec((tm, D_MODEL), lambda i: (i, 0)),
            pl.BlockSpec((1, D_MODEL), const),
            pl.BlockSpec(w1.shape, const),
            pl.BlockSpec(w2.shape, const),
            pl.BlockSpec((1, Q_LORA), const),
            pl.BlockSpec((1, KV_LORA), const),
            pl.BlockSpec(wuq.shape, const),
            pl.BlockSpec(wukv.shape, const),
            pl.BlockSpec((3, tm, LANES), lambda i: (0, i, 0)),
        ],
        out_specs=[
            pl.BlockSpec((tm, A_WIDTH), lambda i: (i, 0)),
            pl.BlockSpec((tm, IDX_HEADS * IDX_DIM), lambda i: (i, 0)),
            pl.BlockSpec((6, tm, LANES), lambda i: (0, i, 0)),
            pl.BlockSpec((tm, LANES), lambda i: (i, 0)),
            pl.BlockSpec((1, B_HEADS, tm, LANES), lambda i: (i // nt, 0, i % nt, 0)),
            pl.BlockSpec((1, B_HEADS, tm, LANES), lambda i: (i // nt, 0, i % nt, 0)),
            pl.BlockSpec((1, B_HEADS // 2, tm, LANES), lambda i: (i // nt, 0, i % nt, 0)),
        ],
        out_shape=[
            jax.ShapeDtypeStruct((t, A_WIDTH), BF16),
            jax.ShapeDtypeStruct((t, IDX_HEADS * IDX_DIM), BF16),
            jax.ShapeDtypeStruct((6, t, LANES), BF16),
            jax.ShapeDtypeStruct((t, LANES), F32),
            jax.ShapeDtypeStruct((bsz, B_HEADS, seq, LANES), BF16),
            jax.ShapeDtypeStruct((bsz, B_HEADS, seq, LANES), BF16),
            jax.ShapeDtypeStruct((bsz, B_HEADS // 2, seq, LANES), BF16),
        ],
        compiler_params=pltpu.CompilerParams(
            dimension_semantics=("arbitrary",), vmem_limit_bytes=VMEM_LIMIT),
        name="proj",
    )(x2d, g, w1, w2, gq, gkv, wuq, wukv, rope)


def _dsa_kernel(aq_ref, iq_ref, iw_ref, kvi_ref, bias_ref, out_ref,
                key_ref, m_ref, lg_ref, mx_ref, acc_ref, *, k_top):
    i = pl.program_id(1)
    n2 = i + 1
    kf = float(k_top)
    nrow = NPAIR * QB

    row = lax.broadcasted_iota(I32, (KC, QB), 0)
    col = lax.broadcasted_iota(I32, (KC, QB), 1)

    def chunk(c):
        return pl.ds(pl.multiple_of(c * KC, KC), KC)

    iq = iq_ref[...]
    iqs = jnp.concatenate([iq[:, :LANES], iq[:, LANES:]], axis=0)
    iw_t = iw_ref[...].T
    w0, w1, w2, w3 = (iw_t[h:h + 1, :] for h in range(IDX_HEADS))

    def score_body(c, carry):
        t0 = lax.dot_general(kvi_ref[G_IK0, chunk(c), :], iqs, NT_DIMS,
                             preferred_element_type=F32)
        t1 = lax.dot_general(kvi_ref[G_IK1, chunk(c), :], iqs, NT_DIMS,
                             preferred_element_type=F32)
        s = (w0 * jnp.maximum(t0[:, :QB], 0.0) + w1 * jnp.maximum(t1[:, :QB], 0.0)
             + w2 * jnp.maximum(t0[:, QB:], 0.0) + w3 * jnp.maximum(t1[:, QB:], 0.0))
        b = lax.bitcast_convert_type(s, I32)
        sk = jnp.where(b < 0, b ^ 0x7FFFFFFF, b)
        sk = jnp.where(c * KC + row > i * QB + col, INT_MIN, sk)
        key_ref[chunk(c), :] = sk
        return carry

    lax.fori_loop(0, n2, score_body, 0)

    def count(cand, strict):
        cb = jnp.broadcast_to(cand, (SUBLANES, QB))

        def body(c, accs):
            start = pl.multiple_of(c * KC, KC)
            accs = list(accs)
            for r in range(KC // SUBLANES):
                blk = key_ref[pl.ds(start + SUBLANES * r, SUBLANES), :]
                hit = (blk > cb) if strict else (blk >= cb)
                accs[r % 4] = accs[r % 4] + jnp.where(hit, 1.0, 0.0)
            return tuple(accs)

        z = jnp.zeros((SUBLANES, QB), F32)
        a = lax.fori_loop(0, n2, body, (z, z, z, z))
        return jnp.sum((a[0] + a[1]) + (a[2] + a[3]), axis=0, keepdims=True)

    c_nonneg = count(jnp.zeros((1, QB), I32), False)
    base = jnp.where(c_nonneg >= kf, 0, INT_MIN).astype(I32)

    def bit_body(it, base):
        cand = base + lax.shift_left(jnp.int32(1), 30 - it)
        return jnp.where(count(cand, False) >= kf, cand, base)

    tau = lax.fori_loop(0, 31, bit_body, base)
    need = kf - count(tau, True)

    ltri = jnp.where(row >= col, 1.0, 0.0).astype(BF16)

    def mask_body(c, carry):
        blk = key_ref[chunk(c), :]
        eq = blk == tau
        pref = jnp.dot(ltri, jnp.where(eq, 1.0, 0.0).astype(BF16),
                       preferred_element_type=F32)
        sel = (blk > tau) | (eq & (pref + carry <= need))
        sel = sel & (c * KC + row <= i * QB + col)
        m_ref[c] = jnp.where(sel, jnp.inf, NEG_INF).T
        return carry + pref[KC - 1:KC, :]

    lax.fori_loop(0, n2, mask_body, jnp.zeros((1, QB), F32))

    aq = aq_ref[...]
    qs = jnp.concatenate([aq[:, j * LANES:(j + 1) * LANES] for j in range(NPAIR)], axis=0)
    mx_ref[...] = jnp.full(mx_ref.shape, -jnp.inf, F32)

    def logits_chunk(c2, bias_pos):
        mc = m_ref[c2]
        for r, g in ((0, G_KK0), (1, G_KK1)):
            lg = lax.dot_general(qs, kvi_ref[g, chunk(c2), :], NT_DIMS,
                                 preferred_element_type=F32)
            if bias_pos is not None:
                lg = lg + bias_ref[bias_pos, r]
            for j in range(NPAIR):
                rows = slice(j * QB, (j + 1) * QB)
                lj = jnp.minimum(lg[rows], mc)
                lg_ref[r, c2, rows, :] = lj
                mrows = slice(r * nrow + j * QB, r * nrow + (j + 1) * QB)
                mx_ref[mrows, :] = jnp.maximum(mx_ref[mrows, :],
                                               jnp.maximum(lj[:, :LANES], lj[:, LANES:]))

    def far_body(c2, carry):
        logits_chunk(c2, None)
        return carry

    lax.fori_loop(0, n2 - 2, far_body, 0)

    @pl.when(n2 >= 2)
    def _():
        logits_chunk(n2 - 2, 0)

    logits_chunk(n2 - 1, 1)

    mrow = jnp.max(mx_ref[...], axis=1, keepdims=True)
    acc_ref[...] = jnp.zeros(acc_ref.shape, F32)

    def pv_body(c2, carry):
        for r, g in ((0, G_VA), (1, G_VB)):
            p = jnp.exp(lg_ref[r, c2] - mrow[r * nrow:(r + 1) * nrow])
            acc_ref[r * nrow:(r + 1) * nrow, :] += jnp.dot(
                p.astype(BF16), kvi_ref[g, chunk(c2), :], preferred_element_type=F32)
        return carry

    lax.fori_loop(0, n2, pv_body, 0)

    lane = lax.broadcasted_iota(I32, (QB, LANES), 1)
    for j in range(NPAIR):
        oe = acc_ref[j * QB:(j + 1) * QB, :]
        oo = acc_ref[nrow + j * QB:nrow + (j + 1) * QB, :]
        out = jnp.where(lane < A_HEAD_DIM, oe / oe[:, A_HEAD_DIM:A_HEAD_DIM + 1], oo / oo[:, 0:1])
        out_ref[:, j * LANES:(j + 1) * LANES] = out.astype(BF16)


def _dsa(aq, iq, iw, kvi, bias, bsz, seq, k_top):
    t = aq.shape[0]
    nq = seq // QB
    return pl.pallas_call(
        functools.partial(_dsa_kernel, k_top=k_top),
        grid=(bsz, nq),
        in_specs=[
            pl.BlockSpec((QB, A_WIDTH), lambda b, i: (b * nq + i, 0)),
            pl.BlockSpec((QB, IDX_HEADS * IDX_DIM), lambda b, i: (b * nq + i, 0)),
            pl.BlockSpec((QB, LANES), lambda b, i: (b * nq + i, 0)),
            pl.BlockSpec((6, seq, LANES), lambda b, i: (0, b, 0)),
            pl.BlockSpec(bias.shape, lambda b, i: (0, 0, 0, 0)),
        ],
        out_specs=pl.BlockSpec((QB, A_WIDTH), lambda b, i: (b * nq + i, 0)),
        out_shape=jax.ShapeDtypeStruct((t, A_WIDTH), BF16),
        scratch_shapes=[
            pltpu.VMEM((seq, QB), I32),
            pltpu.VMEM((nq, QB, KC), F32),
            pltpu.VMEM((2, nq, NPAIR * QB, KC), F32),
            pltpu.VMEM((A_HEADS * QB, LANES), F32),
            pltpu.VMEM((A_HEADS * QB, LANES), F32),
        ],
        compiler_params=pltpu.CompilerParams(
            dimension_semantics=("arbitrary", "arbitrary"), vmem_limit_bytes=VMEM_LIMIT),
        name="dsa",
    )(aq, iq, iw, kvi, bias)


def _mla_kernel(q_ref, k_ref, v_ref, out_ref, lg_ref, mx_ref, ls_ref, acc_ref):
    tq = q_ref.shape[2]
    i = pl.program_id(2)
    lane = lax.broadcasted_iota(I32, (tq, LANES), 1)

    def chunk(c):
        return pl.ds(pl.multiple_of(c * tq, tq), tq)

    def lane_fold(x, op):
        part = x[:, :LANES]
        for s in range(1, tq // LANES):
            part = op(part, x[:, s * LANES:(s + 1) * LANES])
        return part

    mx_ref[...] = jnp.full(mx_ref.shape, -jnp.inf, F32)

    def logits(c, diag):
        for e in range(2):
            lg = lax.dot_general(q_ref[0, e], k_ref[0, e, chunk(c), :], NT_DIMS,
                                 preferred_element_type=F32)
            if diag:
                row = lax.broadcasted_iota(I32, (tq, tq), 0)
                col = lax.broadcasted_iota(I32, (tq, tq), 1)
                lg = jnp.where(col <= row, lg, NEG_INF)
            lg_ref[e, c] = lg
            mx_ref[e] = jnp.maximum(mx_ref[e], lane_fold(lg, jnp.maximum))

    def p1(c, carry):
        logits(c, False)
        return carry

    lax.fori_loop(0, i, p1, 0)
    logits(i, True)
    m = [jnp.max(mx_ref[e], axis=1, keepdims=True) for e in range(2)]
    ls_ref[...] = jnp.zeros(ls_ref.shape, F32)
    acc_ref[...] = jnp.zeros(acc_ref.shape, F32)

    def p2(c, carry):
        v = v_ref[0, 0, chunk(c), :]
        for e in range(2):
            p = jnp.exp(lg_ref[e, c] - m[e])
            ls_ref[e] += lane_fold(p, jnp.add)
            acc_ref[e] += jnp.dot(p.astype(BF16), v, preferred_element_type=F32)
        return carry

    lax.fori_loop(0, i + 1, p2, 0)
    outs = [acc_ref[e] / jnp.sum(ls_ref[e], axis=1, keepdims=True) for e in range(2)]
    out_ref[0] = jnp.where(lane < B_V_DIM, outs[0], outs[1]).astype(BF16)


def _mla(qm, km, vm, tq):
    bsz, _, seq, _ = qm.shape
    return pl.pallas_call(
        _mla_kernel,
        grid=(bsz, B_HEADS // 2, seq // tq),
        in_specs=[
            pl.BlockSpec((1, 2, tq, LANES), lambda b, p, i: (b, p, i, 0)),
            pl.BlockSpec((1, 2, seq, LANES), lambda b, p, i: (b, p, 0, 0)),
            pl.BlockSpec((1, 1, seq, LANES), lambda b, p, i: (b, p, 0, 0)),
        ],
        out_specs=pl.BlockSpec((1, tq, LANES), lambda b, p, i: (b, i, p)),
        out_shape=jax.ShapeDtypeStruct((bsz, seq, B_WIDTH), BF16),
        scratch_shapes=[
            pltpu.VMEM((2, seq // tq, tq, tq), F32),
            pltpu.VMEM((2, tq, LANES), F32),
            pltpu.VMEM((2, tq, LANES), F32),
            pltpu.VMEM((2, tq, LANES), F32),
        ],
        compiler_params=pltpu.CompilerParams(
            dimension_semantics=("arbitrary", "arbitrary", "arbitrary"),
            vmem_limit_bytes=VMEM_LIMIT),
        name="mla",
    )(qm, km, vm)


def _merge_kernel(x_ref, g_ref, wg_ref, bg_ref, ya_ref, yb_ref, wa_ref, wb_ref, wo_ref, o_ref):
    x = x_ref[...]
    hb = _rms(x, g_ref[...]).astype(BF16)
    gl = jnp.dot(hb, wg_ref[...], preferred_element_type=F32) + bg_ref[...]
    gates = 1.0 / (1.0 + jnp.exp(-gl))
    pa = jnp.dot(ya_ref[...], wa_ref[...], preferred_element_type=F32)
    pb = jnp.dot(yb_ref[...], wb_ref[...], preferred_element_type=F32)
    merged = gates[:, :D_MODEL] * pa + gates[:, D_MODEL:] * pb
    o_ref[...] = x + jnp.dot(merged.astype(BF16), wo_ref[...], preferred_element_type=F32)


def _merge(x2d, g, wg, bg, ya, yb, wa, wb, wo, tm):
    t = x2d.shape[0]
    const = lambda i: (0, 0)
    return pl.pallas_call(
        _merge_kernel,
        grid=(t // tm,),
        in_specs=[
            pl.BlockSpec((tm, D_MODEL), lambda i: (i, 0)),
            pl.BlockSpec((1, D_MODEL), const),
            pl.BlockSpec(wg.shape, const),
            pl.BlockSpec((1, N_BRANCHES * D_MODEL), const),
            pl.BlockSpec((tm, A_WIDTH), lambda i: (i, 0)),
            pl.BlockSpec((tm, B_WIDTH), lambda i: (i, 0)),
            pl.BlockSpec(wa.shape, const),
            pl.BlockSpec(wb.shape, const),
            pl.BlockSpec(wo.shape, const),
        ],
        out_specs=pl.BlockSpec((tm, D_MODEL), lambda i: (i, 0)),
        out_shape=jax.ShapeDtypeStruct((t, D_MODEL), F32),
        compiler_params=pltpu.CompilerParams(
            dimension_semantics=("arbitrary",), vmem_limit_bytes=VMEM_LIMIT),
        name="merge",
    )(x2d, g, wg, bg, ya, yb, wa, wb, wo)


def _mlp_kernel(x_ref, g_ref, w1_ref, w2_ref, gf_ref, o_ref, *, final, fc):
    x = x_ref[...]
    hb = _rms(x, g_ref[...]).astype(BF16)
    acc = jnp.zeros(x.shape, F32)
    for j in range(D_FF // fc):
        a = jnp.dot(hb, w1_ref[:, j * fc:(j + 1) * fc], preferred_element_type=F32)
        a = jnp.square(jnp.maximum(a, 0.0)).astype(BF16)
        acc = acc + jnp.dot(a, w2_ref[j * fc:(j + 1) * fc, :], preferred_element_type=F32)
    y = x + acc
    if final:
        y = _rms(y, gf_ref[...])
    o_ref[...] = y


def _mlp(x2d, g, w1, w2, gf, tm, final):
    t = x2d.shape[0]
    const = lambda i: (0, 0)
    return pl.pallas_call(
        functools.partial(_mlp_kernel, final=final, fc=1024),
        grid=(t // tm,),
        in_specs=[
            pl.BlockSpec((tm, D_MODEL), lambda i: (i, 0)),
            pl.BlockSpec((1, D_MODEL), const),
            pl.BlockSpec(w1.shape, const, pipeline_mode=pl.Buffered(1)),
            pl.BlockSpec(w2.shape, const, pipeline_mode=pl.Buffered(1)),
            pl.BlockSpec((1, D_MODEL), const),
        ],
        out_specs=pl.BlockSpec((tm, D_MODEL), lambda i: (i, 0)),
        out_shape=jax.ShapeDtypeStruct((t, D_MODEL), F32),
        compiler_params=pltpu.CompilerParams(
            dimension_semantics=("arbitrary",), vmem_limit_bytes=VMEM_LIMIT),
        name="mlp",
    )(x2d, g, w1, w2, gf)


def _split_w_in(w_in):
    cuts = [0]
    for s in IN_SIZES:
        cuts.append(cuts[-1] + s)
    return [w_in[..., cuts[n]:cuts[n + 1]] for n in range(len(IN_SIZES))]


def _prep_weights(w_in, w_uq, w_ukv):
    depth = w_in.shape[0]
    a_q, a_k, a_v, i_q, i_k, i_w, c_q, c_kv, k_r, w_g = _split_w_in(w_in)
    z64 = jnp.zeros((depth, D_MODEL, 64), F32)
    w1 = jnp.concatenate([a_q * A_HEAD_DIM ** -0.5, i_q * IDX_DIM ** -0.5, c_q], axis=-1)
    w2 = jnp.concatenate([
        i_k, z64, z64, i_k, a_k, z64, z64, a_k, a_v, z64, z64, a_v, c_kv,
        z64, k_r, jnp.zeros((depth, D_MODEL, 32), F32),
        i_w * IDX_HEADS ** -0.5, jnp.zeros((depth, D_MODEL, LANES - IDX_HEADS), F32)], axis=-1)
    assert w2.shape[-1] == W2_COLS
    wuq = w_uq.reshape(depth, Q_LORA, B_HEADS, B_QK_DIM)
    wuq = jnp.pad(wuq, ((0, 0), (0, 0), (0, 0), (0, LANES - B_QK_DIM)))
    wuq = wuq.reshape(depth, Q_LORA, B_HEADS * LANES)
    wukv = w_ukv.reshape(depth, KV_LORA, B_HEADS, B_NOPE_DIM + B_V_DIM)
    wk = jnp.pad(wukv[..., :B_NOPE_DIM], ((0, 0), (0, 0), (0, 0), (0, LANES - B_NOPE_DIM)))
    wk = wk.reshape(depth, KV_LORA, B_HEADS * LANES)
    wv = wukv[..., B_NOPE_DIM:].reshape(depth, KV_LORA, B_HEADS * B_V_DIM)
    wukv = jnp.concatenate([wk, wv], axis=-1)
    return (w1.astype(BF16), w2.astype(BF16), w_g.astype(BF16), wuq.astype(BF16),
            wukv.astype(BF16))


def _rope_tables(positions):
    inv_freq = ROPE_THETA ** (-jnp.arange(0, B_ROPE_DIM, 2, dtype=F32) / B_ROPE_DIM)
    ang = positions.astype(F32).reshape(-1)[:, None] * inv_freq
    cos, sin = jnp.cos(ang), jnp.sin(ang)
    t = ang.shape[0]
    half = B_ROPE_DIM // 2
    one = jnp.ones((t, B_NOPE_DIM), F32)
    z = lambda n: jnp.zeros((t, n), F32)
    cs = jnp.concatenate([one, cos, cos, z(LANES - B_QK_DIM)], axis=1)
    s1 = jnp.concatenate([z(B_NOPE_DIM + half), sin, z(LANES - B_QK_DIM)], axis=1)
    s2 = jnp.concatenate([z(B_NOPE_DIM), -sin, z(half + LANES - B_QK_DIM)], axis=1)
    return jnp.stack([cs, s1, s2])


def _rel_bucket(dist):
    max_exact = REL_BUCKETS // 2
    n = jnp.maximum(dist, 0)
    nf = jnp.maximum(n.astype(F32), 1.0)
    log_b = max_exact + (jnp.log(nf / max_exact) / math.log(REL_MAX_DIST / max_exact)
                         * (REL_BUCKETS - max_exact)).astype(I32)
    return jnp.where(n < max_exact, n, jnp.minimum(log_b, REL_BUCKETS - 1))


def _bias_tables(rel_bias):
    d = jnp.arange(0, 2 * QB, dtype=I32)
    tab = rel_bias[_rel_bucket(d)] - rel_bias[REL_BUCKETS - 1][None, :]
    q = jnp.arange(QB)[:, None]
    s = jnp.arange(KC)[None, :]
    diag = jnp.where((q >= s)[..., None], tab[jnp.clip(q - s, 0, 2 * QB - 1)], 0.0)
    prev = tab[jnp.clip(KC + q - s, 0, 2 * QB - 1)]
    table = jnp.stack([prev, diag])
    table = table.reshape(2, QB, KC, NPAIR, 2)
    table = jnp.transpose(table, (0, 4, 3, 1, 2))
    return table.reshape(2, 2, NPAIR * QB, KC).astype(F32)


def kernel(x, positions, attn_norm_g, w_in, b_gate, q_latent_norm_g, kv_latent_norm_g, w_uq,
           w_ukv, w_branch_a, w_branch_b, w_out, mlp_norm_g, w_ff1, w_ff2, rel_bias,
           final_norm_g):
    bsz, seq, _ = x.shape
    depth = w_in.shape[0]
    t = bsz * seq
    k_top = min(TOPK_MAX, seq // 4)
    tm = min(512, seq)
    tq = min(512, seq)

    w1, w2, wg, wuq, wukv = _prep_weights(w_in, w_uq, w_ukv)
    wa, wb, wo = w_branch_a.astype(BF16), w_branch_b.astype(BF16), w_out.astype(BF16)
    wf1, wf2 = w_ff1.astype(BF16), w_ff2.astype(BF16)
    rope = _rope_tables(positions)
    bias = _bias_tables(rel_bias)
    gf = final_norm_g.reshape(1, D_MODEL)

    x2d = x.reshape(t, D_MODEL)
    for l in range(depth):
        g_attn = attn_norm_g[l].reshape(1, D_MODEL)
        aq, iq, kvi, iw, qm, km, vm = _proj(
            x2d, g_attn, w1[l], w2[l], q_latent_norm_g[l].reshape(1, Q_LORA),
            kv_latent_norm_g[l].reshape(1, KV_LORA), wuq[l], wukv[l], rope, bsz, seq, tm)
        ya = _dsa(aq, iq, iw, kvi, bias, bsz, seq, k_top)
        yb = _mla(qm, km, vm, tq).reshape(t, B_WIDTH)
        x2d = _merge(x2d, g_attn, wg[l], b_gate[l].reshape(1, N_BRANCHES * D_MODEL), ya, yb,
                     wa[l], wb[l], wo[l], tm)
        x2d = _mlp(x2d, mlp_norm_g[l].reshape(1, D_MODEL), wf1[l], wf2[l], gf, tm,
                   final=(l == depth - 1))
    return x2d.reshape(bsz, seq, D_MODEL)
```

```python
import functools
import math

import jax
import jax.numpy as jnp
from jax import lax
from jax.experimental import pallas as pl
from jax.experimental.pallas import tpu as pltpu

F32 = jnp.float32
BF16 = jnp.bfloat16
I32 = jnp.int32

D_MODEL = 1024
A_HEADS = 8
A_HEAD_DIM = 64
A_WIDTH = A_HEADS * A_HEAD_DIM
IDX_HEADS = 4
IDX_DIM = 64
TOPK_MAX = 256
B_HEADS = 8
B_NOPE_DIM = 64
B_ROPE_DIM = 32
B_QK_DIM = B_NOPE_DIM + B_ROPE_DIM
B_V_DIM = 64
B_WIDTH = B_HEADS * B_V_DIM
Q_LORA = 256
KV_LORA = 128
ROPE_THETA = 10000.0
REL_BUCKETS = 32
REL_MAX_DIST = 128
N_BRANCHES = 2
D_FF = 4 * D_MODEL
EPS = 1e-6
NEG_INF = -1e30
IN_SIZES = (A_WIDTH, A_HEAD_DIM, A_HEAD_DIM, IDX_HEADS * IDX_DIM, IDX_DIM, IDX_HEADS,
            Q_LORA, KV_LORA, B_ROPE_DIM, N_BRANCHES * D_MODEL)

LANES = 128
SUBLANES = 8
QB = 2 * LANES
KC = QB
NPAIR = A_HEADS // 2
INT_MIN = -2 ** 31
G_IK0, G_IK1, G_KK0, G_KK1, G_VA, G_VB = range(6)
W2_COLS = 9 * LANES
VMEM_LIMIT = 56 * 1024 * 1024

NT_DIMS = (((1,), (1,)), ((), ()))


def _rms(x, g):
    return x * lax.rsqrt(jnp.mean(x * x, axis=-1, keepdims=True) + EPS) * g


def _proj_kernel(x_ref, g_ref, w1_ref, w2_ref, gq_ref, gkv_ref, wuq_ref, wukv_ref, rope_ref,
                 aq_ref, iq_ref, kvi_ref, iw_ref, qm_ref, km_ref, vm_ref):
    tm = x_ref.shape[0]
    hb = _rms(x_ref[...], g_ref[...]).astype(BF16)
    z1 = jnp.dot(hb, w1_ref[...], preferred_element_type=F32)
    z2 = jnp.dot(hb, w2_ref[...], preferred_element_type=F32)
    aq_ref[...] = z1[:, :A_WIDTH].astype(BF16)
    iq_ref[...] = z1[:, A_WIDTH:A_WIDTH + IDX_HEADS * IDX_DIM].astype(BF16)
    lane = lax.broadcasted_iota(I32, (tm, LANES), 1)
    for gi in range(6):
        blk = z2[:, gi * LANES:(gi + 1) * LANES]
        if gi == G_VA:
            blk = jnp.where(lane == A_HEAD_DIM, 1.0, blk)
        if gi == G_VB:
            blk = jnp.where(lane == 0, 1.0, blk)
        kvi_ref[gi] = blk.astype(BF16)
    iw_ref[...] = z2[:, 8 * LANES:9 * LANES]

    cs, s1, s2 = rope_ref[0], rope_ref[1], rope_ref[2]

    def rope(u):
        return u * cs + pltpu.roll(u, 16, 1) * s1 + pltpu.roll(u, LANES - 16, 1) * s2

    cq = z1[:, A_WIDTH + IDX_HEADS * IDX_DIM:]
    qup = jnp.dot(_rms(cq, gq_ref[...]).astype(BF16), wuq_ref[...], preferred_element_type=F32)
    scale = B_QK_DIM ** -0.5
    for h in range(B_HEADS):
        qm_ref[0, h] = (rope(qup[:, h * LANES:(h + 1) * LANES]) * scale).astype(BF16)
    ckv = z2[:, 6 * LANES:7 * LANES]
    kvup = jnp.dot(_rms(ckv, gkv_ref[...]).astype(BF16), wukv_ref[...],
                   preferred_element_type=F32)
    kr = rope(z2[:, 7 * LANES:8 * LANES])
    for h in range(B_HEADS):
        km_ref[0, h] = (kvup[:, h * LANES:(h + 1) * LANES] + kr).astype(BF16)
    for p in range(B_HEADS // 2):
        c0 = B_HEADS * LANES + p * LANES
        vm_ref[0, p] = kvup[:, c0:c0 + LANES].astype(BF16)


def _proj(x2d, g, w1, w2, gq, gkv, wuq, wukv, rope, bsz, seq, tm):
    t = x2d.shape[0]
    nt = seq // tm
    const = lambda i: (0, 0)
    return pl.pallas_call(
        _proj_kernel,
        grid=(t // tm,),
        in_specs=[
            pl.BlockSpec((tm, D_MODEL), lambda i: (i, 0)),
            pl.BlockSpec((1, D_MODEL), const),
            pl.BlockSpec(w1.shape, const),
            pl.BlockSpec(w2.shape, const),
            pl.BlockSpec((1, Q_LORA), const),
            pl.BlockSpec((1, KV_LORA), const),
            pl.BlockSpec(wuq.shape, const),
            pl.BlockSpec(wukv.shape, const),
            pl.BlockSpec((3, tm, LANES), lambda i: (0, i, 0)),
        ],
        out_specs=[
            pl.BlockSpec((tm, A_WIDTH), lambda i: (i, 0)),
            pl.BlockSpec((tm, IDX_HEADS * IDX_DIM), lambda i: (i, 0)),
            pl.BlockSpec((6, tm, LANES), lambda i: (0, i, 0)),
            pl.BlockSpec((tm, LANES), lambda i: (i, 0)),
            pl.BlockSpec((1, B_HEADS, tm, LANES), lambda i: (i // nt, 0, i % nt, 0)),
            pl.BlockSpec((1, B_HEADS, tm, LANES), lambda i: (i // nt, 0, i % nt, 0)),
            pl.BlockSpec((1, B_HEADS // 2, tm, LANES), lambda i: (i // nt, 0, i % nt, 0)),
        ],
        out_shape=[
            jax.ShapeDtypeStruct((t, A_WIDTH), BF16),
            jax.ShapeDtypeStruct((t, IDX_HEADS * IDX_DIM), BF16),
            jax.ShapeDtypeStruct((6, t, LANES), BF16),
            jax.ShapeDtypeStruct((t, LANES), F32),
            jax.ShapeDtypeStruct((bsz, B_HEADS, seq, LANES), BF16),
            jax.ShapeDtypeStruct((bsz, B_HEADS, seq, LANES), BF16),
            jax.ShapeDtypeStruct((bsz, B_HEADS // 2, seq, LANES), BF16),
        ],
        compiler_params=pltpu.CompilerParams(
            dimension_semantics=("arbitrary",), vmem_limit_bytes=VMEM_LIMIT),
        name="proj",
    )(x2d, g, w1, w2, gq, gkv, wuq, wukv, rope)


def _dsa_kernel(aq_ref, iq_ref, iw_ref, kvi_ref, bias_ref, out_ref,
                key_ref, tie_ref, lg_ref, mx_ref, acc_ref, *, k_top):
    i = pl.program_id(1)
    n2 = i + 1
    kf = float(k_top)
    nrow = NPAIR * QB

    row = lax.broadcasted_iota(I32, (KC, QB), 0)
    col = lax.broadcasted_iota(I32, (KC, QB), 1)

    def chunk(c):
        return pl.ds(pl.multiple_of(c * KC, KC), KC)

    iq = iq_ref[...]
    iqs = jnp.concatenate([iq[:, :LANES], iq[:, LANES:]], axis=0)
    iw_t = iw_ref[...].T
    w0, w1, w2, w3 = (iw_t[h:h + 1, :] for h in range(IDX_HEADS))

    def score_body(c, carry):
        t0 = lax.dot_general(kvi_ref[G_IK0, chunk(c), :], iqs, NT_DIMS,
                             preferred_element_type=F32)
        t1 = lax.dot_general(kvi_ref[G_IK1, chunk(c), :], iqs, NT_DIMS,
                             preferred_element_type=F32)
        s = (w0 * jnp.maximum(t0[:, :QB], 0.0) + w1 * jnp.maximum(t1[:, :QB], 0.0)
             + w2 * jnp.maximum(t0[:, QB:], 0.0) + w3 * jnp.maximum(t1[:, QB:], 0.0))
        b = lax.bitcast_convert_type(s, I32)
        sk = jnp.where(b < 0, b ^ 0x7FFFFFFF, b)
        sk = jnp.where(c * KC + row > i * QB + col, INT_MIN, sk)
        key_ref[chunk(c), :] = sk
        return carry

    lax.fori_loop(0, n2, score_body, 0)

    def count(cand, strict):
        cb = jnp.broadcast_to(cand, (SUBLANES, QB))

        def body(c, accs):
            keys = key_ref[chunk(c), :]
            accs = list(accs)
            for r in range(KC // SUBLANES):
                blk = keys[r * SUBLANES:(r + 1) * SUBLANES, :]
                hit = (blk > cb) if strict else (blk >= cb)
                accs[r % 4] = accs[r % 4] + jnp.where(hit, 1.0, 0.0)
            return tuple(accs)

        z = jnp.zeros((SUBLANES, QB), F32)
        a = lax.fori_loop(0, n2, body, (z, z, z, z))
        return jnp.sum((a[0] + a[1]) + (a[2] + a[3]), axis=0, keepdims=True)

    c_nonneg = count(jnp.zeros((1, QB), I32), False)
    base = jnp.where(c_nonneg >= kf, 0, INT_MIN).astype(I32)

    def bit_body(it, base):
        cand = base + lax.shift_left(jnp.int32(1), 30 - it)
        return jnp.where(count(cand, False) >= kf, cand, base)

    tau = lax.fori_loop(0, 31, bit_body, base)
    need = kf - count(tau, True)

    ltri = jnp.where(row >= col, 1.0, 0.0).astype(BF16)

    def mask_chunk(c):
        blk = key_ref[chunk(c), :]
        eq = blk == tau
        pref = jnp.dot(ltri, jnp.where(eq, 1.0, 0.0).astype(BF16),
                       preferred_element_type=F32)
        carry = tie_ref[...]
        sel = (blk > tau) | (eq & (pref + carry <= need))
        sel = sel & (c * KC + row <= i * QB + col)
        tie_ref[...] = carry + pref[KC - 1:KC, :]
        return jnp.where(sel, jnp.inf, NEG_INF).T

    tie_ref[...] = jnp.zeros(tie_ref.shape, F32)

    aq = aq_ref[...]
    qs = jnp.concatenate([aq[:, j * LANES:(j + 1) * LANES] for j in range(NPAIR)], axis=0)
    mx_ref[...] = jnp.full(mx_ref.shape, -jnp.inf, F32)

    def logits_chunk(c2, bias_pos):
        mc = mask_chunk(c2)
        for r, g in ((0, G_KK0), (1, G_KK1)):
            lg = lax.dot_general(qs, kvi_ref[g, chunk(c2), :], NT_DIMS,
                                 preferred_element_type=F32)
            if bias_pos is not None:
                lg = lg + bias_ref[bias_pos, r]
            for j in range(NPAIR):
                rows = slice(j * QB, (j + 1) * QB)
                lj = jnp.minimum(lg[rows], mc)
                lg_ref[r, c2, rows, :] = lj
                mrows = slice(r * nrow + j * QB, r * nrow + (j + 1) * QB)
                mx_ref[mrows, :] = jnp.maximum(mx_ref[mrows, :],
                                               jnp.maximum(lj[:, :LANES], lj[:, LANES:]))

    def far_body(c2, carry):
        logits_chunk(c2, None)
        return carry

    lax.fori_loop(0, n2 - 2, far_body, 0)

    @pl.when(n2 >= 2)
    def _():
        logits_chunk(n2 - 2, 0)

    logits_chunk(n2 - 1, 1)

    mrow = jnp.max(mx_ref[...], axis=1, keepdims=True)
    acc_ref[...] = jnp.zeros(acc_ref.shape, F32)

    def pv_body(c2, carry):
        for r, g in ((0, G_VA), (1, G_VB)):
            p = jnp.exp(lg_ref[r, c2] - mrow[r * nrow:(r + 1) * nrow])
            acc_ref[r * nrow:(r + 1) * nrow, :] += jnp.dot(
                p.astype(BF16), kvi_ref[g, chunk(c2), :], preferred_element_type=F32)
        return carry

    lax.fori_loop(0, n2, pv_body, 0)

    lane = lax.broadcasted_iota(I32, (QB, LANES), 1)
    for j in range(NPAIR):
        oe = acc_ref[j * QB:(j + 1) * QB, :]
        oo = acc_ref[nrow + j * QB:nrow + (j + 1) * QB, :]
        out = jnp.where(lane < A_HEAD_DIM, oe / oe[:, A_HEAD_DIM:A_HEAD_DIM + 1], oo / oo[:, 0:1])
        out_ref[:, j * LANES:(j + 1) * LANES] = out.astype(BF16)


def _dsa(aq, iq, iw, kvi, bias, bsz, seq, k_top):
    t = aq.shape[0]
    nq = seq // QB
    return pl.pallas_call(
        functools.partial(_dsa_kernel, k_top=k_top),
        grid=(bsz, nq),
        in_specs=[
            pl.BlockSpec((QB, A_WIDTH), lambda b, i: (b * nq + i, 0)),
            pl.BlockSpec((QB, IDX_HEADS * IDX_DIM), lambda b, i: (b * nq + i, 0)),
            pl.BlockSpec((QB, LANES), lambda b, i: (b * nq + i, 0)),
            pl.BlockSpec((6, seq, LANES), lambda b, i: (0, b, 0)),
            pl.BlockSpec(bias.shape, lambda b, i: (0, 0, 0, 0)),
        ],
        out_specs=pl.BlockSpec((QB, A_WIDTH), lambda b, i: (b * nq + i, 0)),
        out_shape=jax.ShapeDtypeStruct((t, A_WIDTH), BF16),
        scratch_shapes=[
            pltpu.VMEM((seq, QB), I32),
            pltpu.VMEM((1, QB), F32),
            pltpu.VMEM((2, nq, NPAIR * QB, KC), F32),
            pltpu.VMEM((A_HEADS * QB, LANES), F32),
            pltpu.VMEM((A_HEADS * QB, LANES), F32),
        ],
        compiler_params=pltpu.CompilerParams(
            dimension_semantics=("arbitrary", "arbitrary"), vmem_limit_bytes=VMEM_LIMIT),
        name="dsa",
    )(aq, iq, iw, kvi, bias)


def _mla_kernel(q_ref, k_ref, v_ref, out_ref, lg_ref, mx_ref, ls_ref, acc_ref):
    tq = q_ref.shape[2]
    i = pl.program_id(2)
    lane = lax.broadcasted_iota(I32, (tq, LANES), 1)

    def chunk(c):
        return pl.ds(pl.multiple_of(c * tq, tq), tq)

    def lane_fold(x, op):
        part = x[:, :LANES]
        for s in range(1, tq // LANES):
            part = op(part, x[:, s * LANES:(s + 1) * LANES])
        return part

    mx_ref[...] = jnp.full(mx_ref.shape, -jnp.inf, F32)

    def logits(c, diag):
        for e in range(2):
            lg = lax.dot_general(q_ref[0, e], k_ref[0, e, chunk(c), :], NT_DIMS,
                                 preferred_element_type=F32)
            if diag:
                row = lax.broadcasted_iota(I32, (tq, tq), 0)
                col = lax.broadcasted_iota(I32, (tq, tq), 1)
                lg = jnp.where(col <= row, lg, NEG_INF)
            lg_ref[e, c] = lg
            mx_ref[e] = jnp.maximum(mx_ref[e], lane_fold(lg, jnp.maximum))

    def p1(c, carry):
        logits(c, False)
        return carry

    lax.fori_loop(0, i, p1, 0)
    logits(i, True)
    m = [jnp.max(mx_ref[e], axis=1, keepdims=True) for e in range(2)]
    ls_ref[...] = jnp.zeros(ls_ref.shape, F32)
    acc_ref[...] = jnp.zeros(acc_ref.shape, F32)

    def p2(c, carry):
        v = v_ref[0, 0, chunk(c), :]
        for e in range(2):
            p = jnp.exp(lg_ref[e, c] - m[e])
            ls_ref[e] += lane_fold(p, jnp.add)
            acc_ref[e] += jnp.dot(p.astype(BF16), v, preferred_element_type=F32)
        return carry

    lax.fori_loop(0, i + 1, p2, 0)
    outs = [acc_ref[e] / jnp.sum(ls_ref[e], axis=1, keepdims=True) for e in range(2)]
    out_ref[0] = jnp.where(lane < B_V_DIM, outs[0], outs[1]).astype(BF16)


def _mla(qm, km, vm, tq):
    bsz, _, seq, _ = qm.shape
    return pl.pallas_call(
        _mla_kernel,
        grid=(bsz, B_HEADS // 2, seq // tq),
        in_specs=[
            pl.BlockSpec((1, 2, tq, LANES), lambda b, p, i: (b, p, i, 0)),
            pl.BlockSpec((1, 2, seq, LANES), lambda b, p, i: (b, p, 0, 0)),
            pl.BlockSpec((1, 1, seq, LANES), lambda b, p, i: (b, p, 0, 0)),
        ],
        out_specs=pl.BlockSpec((1, tq, LANES), lambda b, p, i: (b, i, p)),
        out_shape=jax.ShapeDtypeStruct((bsz, seq, B_WIDTH), BF16),
        scratch_shapes=[
            pltpu.VMEM((2, seq // tq, tq, tq), F32),
            pltpu.VMEM((2, tq, LANES), F32),
            pltpu.VMEM((2, tq, LANES), F32),
            pltpu.VMEM((2, tq, LANES), F32),
        ],
        compiler_params=pltpu.CompilerParams(
            dimension_semantics=("arbitrary", "arbitrary", "arbitrary"),
            vmem_limit_bytes=VMEM_LIMIT),
        name="mla",
    )(qm, km, vm)


def _merge_kernel(x_ref, g_ref, wg_ref, bg_ref, ya_ref, yb_ref, wa_ref, wb_ref, wo_ref, o_ref):
    x = x_ref[...]
    hb = _rms(x, g_ref[...]).astype(BF16)
    gl = jnp.dot(hb, wg_ref[...], preferred_element_type=F32) + bg_ref[...]
    gates = 1.0 / (1.0 + jnp.exp(-gl))
    pa = jnp.dot(ya_ref[...], wa_ref[...], preferred_element_type=F32)
    pb = jnp.dot(yb_ref[...], wb_ref[...], preferred_element_type=F32)
    merged = gates[:, :D_MODEL] * pa + gates[:, D_MODEL:] * pb
    o_ref[...] = x + jnp.dot(merged.astype(BF16), wo_ref[...], preferred_element_type=F32)


def _merge(x2d, g, wg, bg, ya, yb, wa, wb, wo, tm):
    t = x2d.shape[0]
    const = lambda i: (0, 0)
    return pl.pallas_call(
        _merge_kernel,
        grid=(t // tm,),
        in_specs=[
            pl.BlockSpec((tm, D_MODEL), lambda i: (i, 0)),
            pl.BlockSpec((1, D_MODEL), const),
            pl.BlockSpec(wg.shape, const),
            pl.BlockSpec((1, N_BRANCHES * D_MODEL), const),
            pl.BlockSpec((tm, A_WIDTH), lambda i: (i, 0)),
            pl.BlockSpec((tm, B_WIDTH), lambda i: (i, 0)),
            pl.BlockSpec(wa.shape, const),
            pl.BlockSpec(wb.shape, const),
            pl.BlockSpec(wo.shape, const),
        ],
        out_specs=pl.BlockSpec((tm, D_MODEL), lambda i: (i, 0)),
        out_shape=jax.ShapeDtypeStruct((t, D_MODEL), F32),
        compiler_params=pltpu.CompilerParams(
            dimension_semantics=("arbitrary",), vmem_limit_bytes=VMEM_LIMIT),
        name="merge",
    )(x2d, g, wg, bg, ya, yb, wa, wb, wo)


def _mlp_kernel(x_ref, g_ref, w1_ref, w2_ref, gf_ref, o_ref, *, final, fc):
    x = x_ref[...]
    hb = _rms(x, g_ref[...]).astype(BF16)
    acc = jnp.zeros(x.shape, F32)
    for j in range(D_FF // fc):
        a = jnp.dot(hb, w1_ref[:, j * fc:(j + 1) * fc], preferred_element_type=F32)
        a = jnp.square(jnp.maximum(a, 0.0)).astype(BF16)
        acc = acc + jnp.dot(a, w2_ref[j * fc:(j + 1) * fc, :], preferred_element_type=F32)
    y = x + acc
    if final:
        y = _rms(y, gf_ref[...])
    o_ref[...] = y


def _mlp(x2d, g, w1, w2, gf, tm, final):
    t = x2d.shape[0]
    const = lambda i: (0, 0)
    return pl.pallas_call(
        functools.partial(_mlp_kernel, final=final, fc=1024),
        grid=(t // tm,),
        in_specs=[
            pl.BlockSpec((tm, D_MODEL), lambda i: (i, 0)),
            pl.BlockSpec((1, D_MODEL), const),
            pl.BlockSpec(w1.shape, const, pipeline_mode=pl.Buffered(1)),
            pl.BlockSpec(w2.shape, const, pipeline_mode=pl.Buffered(1)),
            pl.BlockSpec((1, D_MODEL), const),
        ],
        out_specs=pl.BlockSpec((tm, D_MODEL), lambda i: (i, 0)),
        out_shape=jax.ShapeDtypeStruct((t, D_MODEL), F32),
        compiler_params=pltpu.CompilerParams(
            dimension_semantics=("arbitrary",), vmem_limit_bytes=VMEM_LIMIT),
        name="mlp",
    )(x2d, g, w1, w2, gf)


def _split_w_in(w_in):
    cuts = [0]
    for s in IN_SIZES:
        cuts.append(cuts[-1] + s)
    return [w_in[..., cuts[n]:cuts[n + 1]] for n in range(len(IN_SIZES))]


def _prep_weights(w_in, w_uq, w_ukv):
    depth = w_in.shape[0]
    a_q, a_k, a_v, i_q, i_k, i_w, c_q, c_kv, k_r, w_g = _split_w_in(w_in)
    z64 = jnp.zeros((depth, D_MODEL, 64), F32)
    w1 = jnp.concatenate([a_q * A_HEAD_DIM ** -0.5, i_q * IDX_DIM ** -0.5, c_q], axis=-1)
    w2 = jnp.concatenate([
        i_k, z64, z64, i_k, a_k, z64, z64, a_k, a_v, z64, z64, a_v, c_kv,
        z64, k_r, jnp.zeros((depth, D_MODEL, 32), F32),
        i_w * IDX_HEADS ** -0.5, jnp.zeros((depth, D_MODEL, LANES - IDX_HEADS), F32)], axis=-1)
    assert w2.shape[-1] == W2_COLS
    wuq = w_uq.reshape(depth, Q_LORA, B_HEADS, B_QK_DIM)
    wuq = jnp.pad(wuq, ((0, 0), (0, 0), (0, 0), (0, LANES - B_QK_DIM)))
    wuq = wuq.reshape(depth, Q_LORA, B_HEADS * LANES)
    wukv = w_ukv.reshape(depth, KV_LORA, B_HEADS, B_NOPE_DIM + B_V_DIM)
    wk = jnp.pad(wukv[..., :B_NOPE_DIM], ((0, 0), (0, 0), (0, 0), (0, LANES - B_NOPE_DIM)))
    wk = wk.reshape(depth, KV_LORA, B_HEADS * LANES)
    wv = wukv[..., B_NOPE_DIM:].reshape(depth, KV_LORA, B_HEADS * B_V_DIM)
    wukv = jnp.concatenate([wk, wv], axis=-1)
    return (w1.astype(BF16), w2.astype(BF16), w_g.astype(BF16), wuq.astype(BF16),
            wukv.astype(BF16))


def _rope_tables(positions):
    inv_freq = ROPE_THETA ** (-jnp.arange(0, B_ROPE_DIM, 2, dtype=F32) / B_ROPE_DIM)
    ang = positions.astype(F32).reshape(-1)[:, None] * inv_freq
    cos, sin = jnp.cos(ang), jnp.sin(ang)
    t = ang.shape[0]
    half = B_ROPE_DIM // 2
    one = jnp.ones((t, B_NOPE_DIM), F32)
    z = lambda n: jnp.zeros((t, n), F32)
    cs = jnp.concatenate([one, cos, cos, z(LANES - B_QK_DIM)], axis=1)
    s1 = jnp.concatenate([z(B_NOPE_DIM + half), sin, z(LANES - B_QK_DIM)], axis=1)
    s2 = jnp.concatenate([z(B_NOPE_DIM), -sin, z(half + LANES - B_QK_DIM)], axis=1)
    return jnp.stack([cs, s1, s2])


def _rel_bucket(dist):
    max_exact = REL_BUCKETS // 2
    n = jnp.maximum(dist, 0)
    nf = jnp.maximum(n.astype(F32), 1.0)
    log_b = max_exact + (jnp.log(nf / max_exact) / math.log(REL_MAX_DIST / max_exact)
                         * (REL_BUCKETS - max_exact)).astype(I32)
    return jnp.where(n < max_exact, n, jnp.minimum(log_b, REL_BUCKETS - 1))


def _bias_tables(rel_bias):
    d = jnp.arange(0, 2 * QB, dtype=I32)
    tab = rel_bias[_rel_bucket(d)] - rel_bias[REL_BUCKETS - 1][None, :]
    q = jnp.arange(QB)[:, None]
    s = jnp.arange(KC)[None, :]
    diag = jnp.where((q >= s)[..., None], tab[jnp.clip(q - s, 0, 2 * QB - 1)], 0.0)
    prev = tab[jnp.clip(KC + q - s, 0, 2 * QB - 1)]
    table = jnp.stack([prev, diag])
    table = table.reshape(2, QB, KC, NPAIR, 2)
    table = jnp.transpose(table, (0, 4, 3, 1, 2))
    return table.reshape(2, 2, NPAIR * QB, KC).astype(F32)


def kernel(x, positions, attn_norm_g, w_in, b_gate, q_latent_norm_g, kv_latent_norm_g, w_uq,
           w_ukv, w_branch_a, w_branch_b, w_out, mlp_norm_g, w_ff1, w_ff2, rel_bias,
           final_norm_g):
    bsz, seq, _ = x.shape
    depth = w_in.shape[0]
    t = bsz * seq
    k_top = min(TOPK_MAX, seq // 4)
    tm = min(512, seq)
    tq = min(512, seq)

    w1, w2, wg, wuq, wukv = _prep_weights(w_in, w_uq, w_ukv)
    wa, wb, wo = w_branch_a.astype(BF16), w_branch_b.astype(BF16), w_out.astype(BF16)
    wf1, wf2 = w_ff1.astype(BF16), w_ff2.astype(BF16)
    rope = _rope_tables(positions)
    bias = _bias_tables(rel_bias)
    gf = final_norm_g.reshape(1, D_MODEL)

    x2d = x.reshape(t, D_MODEL)
    for l in range(depth):
        g_attn = attn_norm_g[l].reshape(1, D_MODEL)
        aq, iq, kvi, iw, qm, km, vm = _proj(
            x2d, g_attn, w1[l], w2[l], q_latent_norm_g[l].reshape(1, Q_LORA),
            kv_latent_norm_g[l].reshape(1, KV_LORA), wuq[l], wukv[l], rope, bsz, seq, tm)
        ya = _dsa(aq, iq, iw, kvi, bias, bsz, seq, k_top)
        yb = _mla(qm, km, vm, tq).reshape(t, B_WIDTH)
        x2d = _merge(x2d, g_attn, wg[l], b_gate[l].reshape(1, N_BRANCHES * D_MODEL), ya, yb,
                     wa[l], wb[l], wo[l], tm)
        x2d = _mlp(x2d, mlp_norm_g[l].reshape(1, D_MODEL), wf1[l], wf2[l], gf, tm,
                   final=(l == depth - 1))
    return x2d.reshape(bsz, seq, D_MODEL)
```

```python
import functools
import math

import jax
import jax.numpy as jnp
from jax import lax
from jax.experimental import pallas as pl
from jax.experimental.pallas import tpu as pltpu

F32 = jnp.float32
BF16 = jnp.bfloat16
I32 = jnp.int32
I16 = jnp.int16

D_MODEL = 1024
A_HEADS = 8
A_HEAD_DIM = 64
A_WIDTH = A_HEADS * A_HEAD_DIM
IDX_HEADS = 4
IDX_DIM = 64
TOPK_MAX = 256
B_HEADS = 8
B_NOPE_DIM = 64
B_ROPE_DIM = 32
B_QK_DIM = B_NOPE_DIM + B_ROPE_DIM
B_V_DIM = 64
B_WIDTH = B_HEADS * B_V_DIM
Q_LORA = 256
KV_LORA = 128
ROPE_THETA = 10000.0
REL_BUCKETS = 32
REL_MAX_DIST = 128
N_BRANCHES = 2
D_FF = 4 * D_MODEL
EPS = 1e-6
NEG_INF = -1e30
IN_SIZES = (A_WIDTH, A_HEAD_DIM, A_HEAD_DIM, IDX_HEADS * IDX_DIM, IDX_DIM, IDX_HEADS,
            Q_LORA, KV_LORA, B_ROPE_DIM, N_BRANCHES * D_MODEL)

LANES = 128
SUBLANES = 8
QB = 2 * LANES
KC = QB
NPAIR = A_HEADS // 2
INT_MIN = -2 ** 31
I16_MIN = -2 ** 15
PACKED_ROWS = 2 * SUBLANES
G_IK0, G_IK1, G_KK0, G_KK1, G_VA, G_VB = range(6)
W2_COLS = 9 * LANES
VMEM_LIMIT = 56 * 1024 * 1024

NT_DIMS = (((1,), (1,)), ((), ()))


def _rms(x, g):
    return x * lax.rsqrt(jnp.mean(x * x, axis=-1, keepdims=True) + EPS) * g


def _proj_kernel(x_ref, g_ref, w1_ref, w2_ref, gq_ref, gkv_ref, wuq_ref, wukv_ref, rope_ref,
                 aq_ref, iq_ref, kvi_ref, iw_ref, qm_ref, km_ref, vm_ref):
    tm = x_ref.shape[0]
    hb = _rms(x_ref[...], g_ref[...]).astype(BF16)
    z1 = jnp.dot(hb, w1_ref[...], preferred_element_type=F32)
    z2 = jnp.dot(hb, w2_ref[...], preferred_element_type=F32)
    aq_ref[...] = z1[:, :A_WIDTH].astype(BF16)
    iq_ref[...] = z1[:, A_WIDTH:A_WIDTH + IDX_HEADS * IDX_DIM].astype(BF16)
    lane = lax.broadcasted_iota(I32, (tm, LANES), 1)
    for gi in range(6):
        blk = z2[:, gi * LANES:(gi + 1) * LANES]
        if gi == G_VA:
            blk = jnp.where(lane == A_HEAD_DIM, 1.0, blk)
        if gi == G_VB:
            blk = jnp.where(lane == 0, 1.0, blk)
        kvi_ref[gi] = blk.astype(BF16)
    iw_ref[...] = z2[:, 8 * LANES:9 * LANES]

    cs, s1, s2 = rope_ref[0], rope_ref[1], rope_ref[2]

    def rope(u):
        return u * cs + pltpu.roll(u, 16, 1) * s1 + pltpu.roll(u, LANES - 16, 1) * s2

    cq = z1[:, A_WIDTH + IDX_HEADS * IDX_DIM:]
    qup = jnp.dot(_rms(cq, gq_ref[...]).astype(BF16), wuq_ref[...], preferred_element_type=F32)
    scale = B_QK_DIM ** -0.5
    for h in range(B_HEADS):
        qm_ref[0, h] = (rope(qup[:, h * LANES:(h + 1) * LANES]) * scale).astype(BF16)
    ckv = z2[:, 6 * LANES:7 * LANES]
    kvup = jnp.dot(_rms(ckv, gkv_ref[...]).astype(BF16), wukv_ref[...],
                   preferred_element_type=F32)
    kr = rope(z2[:, 7 * LANES:8 * LANES])
    for h in range(B_HEADS):
        km_ref[0, h] = (kvup[:, h * LANES:(h + 1) * LANES] + kr).astype(BF16)
    for p in range(B_HEADS // 2):
        c0 = B_HEADS * LANES + p * LANES
        vm_ref[0, p] = kvup[:, c0:c0 + LANES].astype(BF16)


def _proj(x2d, g, w1, w2, gq, gkv, wuq, wukv, rope, bsz, seq, tm):
    t = x2d.shape[0]
    nt = seq // tm
    const = lambda i: (0, 0)
    return pl.pallas_call(
        _proj_kernel,
        grid=(t // tm,),
        in_specs=[
            pl.BlockSpec((tm, D_MODEL), lambda i: (i, 0)),
            pl.BlockSpec((1, D_MODEL), const),
            pl.BlockSpec(w1.shape, const),
            pl.BlockSpec(w2.shape, const),
            pl.BlockSpec((1, Q_LORA), const),
            pl.BlockSpec((1, KV_LORA), const),
            pl.BlockSpec(wuq.shape, const),
            pl.BlockSpec(wukv.shape, const),
            pl.BlockSpec((3, tm, LANES), lambda i: (0, i, 0)),
        ],
        out_specs=[
            pl.BlockSpec((tm, A_WIDTH), lambda i: (i, 0)),
            pl.BlockSpec((tm, IDX_HEADS * IDX_DIM), lambda i: (i, 0)),
            pl.BlockSpec((6, tm, LANES), lambda i: (0, i, 0)),
            pl.BlockSpec((tm, LANES), lambda i: (i, 0)),
            pl.BlockSpec((1, B_HEADS, tm, LANES), lambda i: (i // nt, 0, i % nt, 0)),
            pl.BlockSpec((1, B_HEADS, tm, LANES), lambda i: (i // nt, 0, i % nt, 0)),
            pl.BlockSpec((1, B_HEADS // 2, tm, LANES), lambda i: (i // nt, 0, i % nt, 0)),
        ],
        out_shape=[
            jax.ShapeDtypeStruct((t, A_WIDTH), BF16),
            jax.ShapeDtypeStruct((t, IDX_HEADS * IDX_DIM), BF16),
            jax.ShapeDtypeStruct((6, t, LANES), BF16),
            jax.ShapeDtypeStruct((t, LANES), F32),
            jax.ShapeDtypeStruct((bsz, B_HEADS, seq, LANES), BF16),
            jax.ShapeDtypeStruct((bsz, B_HEADS, seq, LANES), BF16),
            jax.ShapeDtypeStruct((bsz, B_HEADS // 2, seq, LANES), BF16),
        ],
        compiler_params=pltpu.CompilerParams(
            dimension_semantics=("arbitrary",), vmem_limit_bytes=VMEM_LIMIT),
        name="proj",
    )(x2d, g, w1, w2, gq, gkv, wuq, wukv, rope)


def _dsa_kernel(aq_ref, iq_ref, iw_ref, kvi_ref, bias_ref, out_ref,
                key_ref, hi_ref, lo_ref, tie_ref, lg_ref, mx_ref, acc_ref, *, k_top):
    i = pl.program_id(1)
    n2 = i + 1
    kf = float(k_top)
    nrow = NPAIR * QB

    row = lax.broadcasted_iota(I32, (KC, QB), 0)
    col = lax.broadcasted_iota(I32, (KC, QB), 1)

    def chunk(c):
        return pl.ds(pl.multiple_of(c * KC, KC), KC)

    iq = iq_ref[...]
    iqs = jnp.concatenate([iq[:, :LANES], iq[:, LANES:]], axis=0)
    iw_t = iw_ref[...].T
    w0, w1, w2, w3 = (iw_t[h:h + 1, :] for h in range(IDX_HEADS))

    def score_body(c, carry):
        t0 = lax.dot_general(kvi_ref[G_IK0, chunk(c), :], iqs, NT_DIMS,
                             preferred_element_type=F32)
        t1 = lax.dot_general(kvi_ref[G_IK1, chunk(c), :], iqs, NT_DIMS,
                             preferred_element_type=F32)
        s = (w0 * jnp.maximum(t0[:, :QB], 0.0) + w1 * jnp.maximum(t1[:, :QB], 0.0)
             + w2 * jnp.maximum(t0[:, QB:], 0.0) + w3 * jnp.maximum(t1[:, QB:], 0.0))
        b = lax.bitcast_convert_type(s, I32)
        sk = jnp.where(b < 0, b ^ 0x7FFFFFFF, b)
        sk = jnp.where(c * KC + row > i * QB + col, INT_MIN, sk)
        key_ref[chunk(c), :] = sk
        hi_ref[chunk(c), :] = (sk >> 16).astype(I16)
        lo_ref[chunk(c), :] = ((sk & 0xFFFF) + I16_MIN).astype(I16)
        return carry

    lax.fori_loop(0, n2, score_body, 0)

    def count16(ref, cand):
        cb = jnp.broadcast_to(cand.astype(I16), (PACKED_ROWS, QB))

        def body(c, accs):
            keys = ref[chunk(c), :]
            accs = list(accs)
            for r in range(KC // PACKED_ROWS):
                blk = keys[r * PACKED_ROWS:(r + 1) * PACKED_ROWS, :]
                accs[r % 4] = accs[r % 4] + jnp.where(blk >= cb, jnp.int16(1), jnp.int16(0))
            return tuple(accs)

        z = jnp.zeros((PACKED_ROWS, QB), I16)
        a = lax.fori_loop(0, n2, body, (z, z, z, z))
        tot = (a[0] + a[1]) + (a[2] + a[3])
        return jnp.sum(tot.astype(F32), axis=0, keepdims=True)

    def bisect16(ref):
        def bit_body(it, base):
            cand = base + lax.shift_left(jnp.int32(1), 15 - it)
            return jnp.where(count16(ref, cand) >= kf, cand, base)

        return lax.fori_loop(0, 16, bit_body, jnp.full((1, QB), I16_MIN, I32))

    tau_hi = bisect16(hi_ref)
    th = tau_hi.astype(I16)

    def refine_body(c, carry):
        hi = hi_ref[chunk(c), :]
        lo_ref[chunk(c), :] = jnp.where(
            hi == th, lo_ref[chunk(c), :],
            jnp.where(hi > th, jnp.int16(-I16_MIN - 1), jnp.int16(I16_MIN)))
        return carry

    lax.fori_loop(0, n2, refine_body, 0)
    tau = tau_hi * 65536 + (bisect16(lo_ref) - I16_MIN)

    def count_gt(cand):
        cb = jnp.broadcast_to(cand, (SUBLANES, QB))

        def body(c, accs):
            keys = key_ref[chunk(c), :]
            accs = list(accs)
            for r in range(KC // SUBLANES):
                blk = keys[r * SUBLANES:(r + 1) * SUBLANES, :]
                accs[r % 4] = accs[r % 4] + jnp.where(blk > cb, 1.0, 0.0)
            return tuple(accs)

        z = jnp.zeros((SUBLANES, QB), F32)
        a = lax.fori_loop(0, n2, body, (z, z, z, z))
        return jnp.sum((a[0] + a[1]) + (a[2] + a[3]), axis=0, keepdims=True)

    need = kf - count_gt(tau)

    ltri = jnp.where(row >= col, 1.0, 0.0).astype(BF16)

    def mask_chunk(c):
        blk = key_ref[chunk(c), :]
        eq = blk == tau
        pref = jnp.dot(ltri, jnp.where(eq, 1.0, 0.0).astype(BF16),
                       preferred_element_type=F32)
        carry = tie_ref[...]
        sel = (blk > tau) | (eq & (pref + carry <= need))
        sel = sel & (c * KC + row <= i * QB + col)
        tie_ref[...] = carry + pref[KC - 1:KC, :]
        return jnp.where(sel, jnp.inf, NEG_INF).T

    tie_ref[...] = jnp.zeros(tie_ref.shape, F32)

    aq = aq_ref[...]
    qs = jnp.concatenate([aq[:, j * LANES:(j + 1) * LANES] for j in range(NPAIR)], axis=0)
    mx_ref[...] = jnp.full(mx_ref.shape, -jnp.inf, F32)

    def logits_chunk(c2, bias_pos):
        mc = mask_chunk(c2)
        for r, g in ((0, G_KK0), (1, G_KK1)):
            lg = lax.dot_general(qs, kvi_ref[g, chunk(c2), :], NT_DIMS,
                                 preferred_element_type=F32)
            if bias_pos is not None:
                lg = lg + bias_ref[bias_pos, r]
            for j in range(NPAIR):
                rows = slice(j * QB, (j + 1) * QB)
                lj = jnp.minimum(lg[rows], mc)
                lg_ref[r, c2, rows, :] = lj
                mrows = slice(r * nrow + j * QB, r * nrow + (j + 1) * QB)
                mx_ref[mrows, :] = jnp.maximum(mx_ref[mrows, :],
                                               jnp.maximum(lj[:, :LANES], lj[:, LANES:]))

    def far_body(c2, carry):
        logits_chunk(c2, None)
        return carry

    lax.fori_loop(0, n2 - 2, far_body, 0)

    @pl.when(n2 >= 2)
    def _():
        logits_chunk(n2 - 2, 0)

    logits_chunk(n2 - 1, 1)

    mrow = jnp.max(mx_ref[...], axis=1, keepdims=True)
    acc_ref[...] = jnp.zeros(acc_ref.shape, F32)

    def pv_body(c2, carry):
        for r, g in ((0, G_VA), (1, G_VB)):
            p = jnp.exp(lg_ref[r, c2] - mrow[r * nrow:(r + 1) * nrow])
            acc_ref[r * nrow:(r + 1) * nrow, :] += jnp.dot(
                p.astype(BF16), kvi_ref[g, chunk(c2), :], preferred_element_type=F32)
        return carry

    lax.fori_loop(0, n2, pv_body, 0)

    lane = lax.broadcasted_iota(I32, (QB, LANES), 1)
    for j in range(NPAIR):
        oe = acc_ref[j * QB:(j + 1) * QB, :]
        oo = acc_ref[nrow + j * QB:nrow + (j + 1) * QB, :]
        out = jnp.where(lane < A_HEAD_DIM, oe / oe[:, A_HEAD_DIM:A_HEAD_DIM + 1], oo / oo[:, 0:1])
        out_ref[:, j * LANES:(j + 1) * LANES] = out.astype(BF16)


def _dsa(aq, iq, iw, kvi, bias, bsz, seq, k_top):
    t = aq.shape[0]
    nq = seq // QB
    return pl.pallas_call(
        functools.partial(_dsa_kernel, k_top=k_top),
        grid=(bsz, nq),
        in_specs=[
            pl.BlockSpec((QB, A_WIDTH), lambda b, i: (b * nq + i, 0)),
            pl.BlockSpec((QB, IDX_HEADS * IDX_DIM), lambda b, i: (b * nq + i, 0)),
            pl.BlockSpec((QB, LANES), lambda b, i: (b * nq + i, 0)),
            pl.BlockSpec((6, seq, LANES), lambda b, i: (0, b, 0)),
            pl.BlockSpec(bias.shape, lambda b, i: (0, 0, 0, 0)),
        ],
        out_specs=pl.BlockSpec((QB, A_WIDTH), lambda b, i: (b * nq + i, 0)),
        out_shape=jax.ShapeDtypeStruct((t, A_WIDTH), BF16),
        scratch_shapes=[
            pltpu.VMEM((seq, QB), I32),
            pltpu.VMEM((seq, QB), I16),
            pltpu.VMEM((seq, QB), I16),
            pltpu.VMEM((1, QB), F32),
            pltpu.VMEM((2, nq, NPAIR * QB, KC), F32),
            pltpu.VMEM((A_HEADS * QB, LANES), F32),
            pltpu.VMEM((A_HEADS * QB, LANES), F32),
        ],
        compiler_params=pltpu.CompilerParams(
            dimension_semantics=("arbitrary", "arbitrary"), vmem_limit_bytes=VMEM_LIMIT),
        name="dsa",
    )(aq, iq, iw, kvi, bias)


def _mla_kernel(q_ref, k_ref, v_ref, out_ref, lg_ref, mx_ref, ls_ref, acc_ref):
    tq = q_ref.shape[2]
    i = pl.program_id(2)
    lane = lax.broadcasted_iota(I32, (tq, LANES), 1)

    def chunk(c):
        return pl.ds(pl.multiple_of(c * tq, tq), tq)

    def lane_fold(x, op):
        part = x[:, :LANES]
        for s in range(1, tq // LANES):
            part = op(part, x[:, s * LANES:(s + 1) * LANES])
        return part

    mx_ref[...] = jnp.full(mx_ref.shape, -jnp.inf, F32)

    def logits(c, diag):
        for e in range(2):
            lg = lax.dot_general(q_ref[0, e], k_ref[0, e, chunk(c), :], NT_DIMS,
                                 preferred_element_type=F32)
            if diag:
                row = lax.broadcasted_iota(I32, (tq, tq), 0)
                col = lax.broadcasted_iota(I32, (tq, tq), 1)
                lg = jnp.where(col <= row, lg, NEG_INF)
            lg_ref[e, c] = lg
            mx_ref[e] = jnp.maximum(mx_ref[e], lane_fold(lg, jnp.maximum))

    def p1(c, carry):
        logits(c, False)
        return carry

    lax.fori_loop(0, i, p1, 0)
    logits(i, True)
    m = [jnp.max(mx_ref[e], axis=1, keepdims=True) for e in range(2)]
    ls_ref[...] = jnp.zeros(ls_ref.shape, F32)
    acc_ref[...] = jnp.zeros(acc_ref.shape, F32)

    def p2(c, carry):
        v = v_ref[0, 0, chunk(c), :]
        for e in range(2):
            p = jnp.exp(lg_ref[e, c] - m[e])
            ls_ref[e] += lane_fold(p, jnp.add)
            acc_ref[e] += jnp.dot(p.astype(BF16), v, preferred_element_type=F32)
        return carry

    lax.fori_loop(0, i + 1, p2, 0)
    outs = [acc_ref[e] / jnp.sum(ls_ref[e], axis=1, keepdims=True) for e in range(2)]
    out_ref[0] = jnp.where(lane < B_V_DIM, outs[0], outs[1]).astype(BF16)


def _mla(qm, km, vm, tq):
    bsz, _, seq, _ = qm.shape
    return pl.pallas_call(
        _mla_kernel,
        grid=(bsz, B_HEADS // 2, seq // tq),
        in_specs=[
            pl.BlockSpec((1, 2, tq, LANES), lambda b, p, i: (b, p, i, 0)),
            pl.BlockSpec((1, 2, seq, LANES), lambda b, p, i: (b, p, 0, 0)),
            pl.BlockSpec((1, 1, seq, LANES), lambda b, p, i: (b, p, 0, 0)),
        ],
        out_specs=pl.BlockSpec((1, tq, LANES), lambda b, p, i: (b, i, p)),
        out_shape=jax.ShapeDtypeStruct((bsz, seq, B_WIDTH), BF16),
        scratch_shapes=[
            pltpu.VMEM((2, seq // tq, tq, tq), F32),
            pltpu.VMEM((2, tq, LANES), F32),
            pltpu.VMEM((2, tq, LANES), F32),
            pltpu.VMEM((2, tq, LANES), F32),
        ],
        compiler_params=pltpu.CompilerParams(
            dimension_semantics=("arbitrary", "arbitrary", "arbitrary"),
            vmem_limit_bytes=VMEM_LIMIT),
        name="mla",
    )(qm, km, vm)


def _merge_kernel(x_ref, g_ref, wg_ref, bg_ref, ya_ref, yb_ref, wa_ref, wb_ref, wo_ref, o_ref):
    x = x_ref[...]
    hb = _rms(x, g_ref[...]).astype(BF16)
    gl = jnp.dot(hb, wg_ref[...], preferred_element_type=F32) + bg_ref[...]
    gates = 1.0 / (1.0 + jnp.exp(-gl))
    pa = jnp.dot(ya_ref[...], wa_ref[...], preferred_element_type=F32)
    pb = jnp.dot(yb_ref[...], wb_ref[...], preferred_element_type=F32)
    merged = gates[:, :D_MODEL] * pa + gates[:, D_MODEL:] * pb
    o_ref[...] = x + jnp.dot(merged.astype(BF16), wo_ref[...], preferred_element_type=F32)


def _merge(x2d, g, wg, bg, ya, yb, wa, wb, wo, tm):
    t = x2d.shape[0]
    const = lambda i: (0, 0)
    return pl.pallas_call(
        _merge_kernel,
        grid=(t // tm,),
        in_specs=[
            pl.BlockSpec((tm, D_MODEL), lambda i: (i, 0)),
            pl.BlockSpec((1, D_MODEL), const),
            pl.BlockSpec(wg.shape, const),
            pl.BlockSpec((1, N_BRANCHES * D_MODEL), const),
            pl.BlockSpec((tm, A_WIDTH), lambda i: (i, 0)),
            pl.BlockSpec((tm, B_WIDTH), lambda i: (i, 0)),
            pl.BlockSpec(wa.shape, const),
            pl.BlockSpec(wb.shape, const),
            pl.BlockSpec(wo.shape, const),
        ],
        out_specs=pl.BlockSpec((tm, D_MODEL), lambda i: (i, 0)),
        out_shape=jax.ShapeDtypeStruct((t, D_MODEL), F32),
        compiler_params=pltpu.CompilerParams(
            dimension_semantics=("arbitrary",), vmem_limit_bytes=VMEM_LIMIT),
        name="merge",
    )(x2d, g, wg, bg, ya, yb, wa, wb, wo)


def _mlp_kernel(x_ref, g_ref, w1_ref, w2_ref, gf_ref, o_ref, *, final, fc):
    x = x_ref[...]
    hb = _rms(x, g_ref[...]).astype(BF16)
    acc = jnp.zeros(x.shape, F32)
    for j in range(D_FF // fc):
        a = jnp.dot(hb, w1_ref[:, j * fc:(j + 1) * fc], preferred_element_type=F32)
        a = jnp.square(jnp.maximum(a, 0.0)).astype(BF16)
        acc = acc + jnp.dot(a, w2_ref[j * fc:(j + 1) * fc, :], preferred_element_type=F32)
    y = x + acc
    if final:
        y = _rms(y, gf_ref[...])
    o_ref[...] = y


def _mlp(x2d, g, w1, w2, gf, tm, final):
    t = x2d.shape[0]
    const = lambda i: (0, 0)
    return pl.pallas_call(
        functools.partial(_mlp_kernel, final=final, fc=1024),
        grid=(t // tm,),
        in_specs=[
            pl.BlockSpec((tm, D_MODEL), lambda i: (i, 0)),
            pl.BlockSpec((1, D_MODEL), const),
            pl.BlockSpec(w1.shape, const, pipeline_mode=pl.Buffered(1)),
            pl.BlockSpec(w2.shape, const, pipeline_mode=pl.Buffered(1)),
            pl.BlockSpec((1, D_MODEL), const),
        ],
        out_specs=pl.BlockSpec((tm, D_MODEL), lambda i: (i, 0)),
        out_shape=jax.ShapeDtypeStruct((t, D_MODEL), F32),
        compiler_params=pltpu.CompilerParams(
            dimension_semantics=("arbitrary",), vmem_limit_bytes=VMEM_LIMIT),
        name="mlp",
    )(x2d, g, w1, w2, gf)


def _split_w_in(w_in):
    cuts = [0]
    for s in IN_SIZES:
        cuts.append(cuts[-1] + s)
    return [w_in[..., cuts[n]:cuts[n + 1]] for n in range(len(IN_SIZES))]


def _prep_weights(w_in, w_uq, w_ukv):
    depth = w_in.shape[0]
    a_q, a_k, a_v, i_q, i_k, i_w, c_q, c_kv, k_r, w_g = _split_w_in(w_in)
    z64 = jnp.zeros((depth, D_MODEL, 64), F32)
    w1 = jnp.concatenate([a_q * A_HEAD_DIM ** -0.5, i_q * IDX_DIM ** -0.5, c_q], axis=-1)
    w2 = jnp.concatenate([
        i_k, z64, z64, i_k, a_k, z64, z64, a_k, a_v, z64, z64, a_v, c_kv,
        z64, k_r, jnp.zeros((depth, D_MODEL, 32), F32),
        i_w * IDX_HEADS ** -0.5, jnp.zeros((depth, D_MODEL, LANES - IDX_HEADS), F32)], axis=-1)
    assert w2.shape[-1] == W2_COLS
    wuq = w_uq.reshape(depth, Q_LORA, B_HEADS, B_QK_DIM)
    wuq = jnp.pad(wuq, ((0, 0), (0, 0), (0, 0), (0, LANES - B_QK_DIM)))
    wuq = wuq.reshape(depth, Q_LORA, B_HEADS * LANES)
    wukv = w_ukv.reshape(depth, KV_LORA, B_HEADS, B_NOPE_DIM + B_V_DIM)
    wk = jnp.pad(wukv[..., :B_NOPE_DIM], ((0, 0), (0, 0), (0, 0), (0, LANES - B_NOPE_DIM)))
    wk = wk.reshape(depth, KV_LORA, B_HEADS * LANES)
    wv = wukv[..., B_NOPE_DIM:].reshape(depth, KV_LORA, B_HEADS * B_V_DIM)
    wukv = jnp.concatenate([wk, wv], axis=-1)
    return (w1.astype(BF16), w2.astype(BF16), w_g.astype(BF16), wuq.astype(BF16),
            wukv.astype(BF16))


def _rope_tables(positions):
    inv_freq = ROPE_THETA ** (-jnp.arange(0, B_ROPE_DIM, 2, dtype=F32) / B_ROPE_DIM)
    ang = positions.astype(F32).reshape(-1)[:, None] * inv_freq
    cos, sin = jnp.cos(ang), jnp.sin(ang)
    t = ang.shape[0]
    half = B_ROPE_DIM // 2
    one = jnp.ones((t, B_NOPE_DIM), F32)
    z = lambda n: jnp.zeros((t, n), F32)
    cs = jnp.concatenate([one, cos, cos, z(LANES - B_QK_DIM)], axis=1)
    s1 = jnp.concatenate([z(B_NOPE_DIM + half), sin, z(LANES - B_QK_DIM)], axis=1)
    s2 = jnp.concatenate([z(B_NOPE_DIM), -sin, z(half + LANES - B_QK_DIM)], axis=1)
    return jnp.stack([cs, s1, s2])


def _rel_bucket(dist):
    max_exact = REL_BUCKETS // 2
    n = jnp.maximum(dist, 0)
    nf = jnp.maximum(n.astype(F32), 1.0)
    log_b = max_exact + (jnp.log(nf / max_exact) / math.log(REL_MAX_DIST / max_exact)
                         * (REL_BUCKETS - max_exact)).astype(I32)
    return jnp.where(n < max_exact, n, jnp.minimum(log_b, REL_BUCKETS - 1))


def _bias_tables(rel_bias):
    d = jnp.arange(0, 2 * QB, dtype=I32)
    tab = rel_bias[_rel_bucket(d)] - rel_bias[REL_BUCKETS - 1][None, :]
    q = jnp.arange(QB)[:, None]
    s = jnp.arange(KC)[None, :]
    diag = jnp.where((q >= s)[..., None], tab[jnp.clip(q - s, 0, 2 * QB - 1)], 0.0)
    prev = tab[jnp.clip(KC + q - s, 0, 2 * QB - 1)]
    table = jnp.stack([prev, diag])
    table = table.reshape(2, QB, KC, NPAIR, 2)
    table = jnp.transpose(table, (0, 4, 3, 1, 2))
    return table.reshape(2, 2, NPAIR * QB, KC).astype(F32)


def kernel(x, positions, attn_norm_g, w_in, b_gate, q_latent_norm_g, kv_latent_norm_g, w_uq,
           w_ukv, w_branch_a, w_branch_b, w_out, mlp_norm_g, w_ff1, w_ff2, rel_bias,
           final_norm_g):
    bsz, seq, _ = x.shape
    depth = w_in.shape[0]
    t = bsz * seq
    k_top = min(TOPK_MAX, seq // 4)
    tm = min(512, seq)
    tq = min(512, seq)

    w1, w2, wg, wuq, wukv = _prep_weights(w_in, w_uq, w_ukv)
    wa, wb, wo = w_branch_a.astype(BF16), w_branch_b.astype(BF16), w_out.astype(BF16)
    wf1, wf2 = w_ff1.astype(BF16), w_ff2.astype(BF16)
    rope = _rope_tables(positions)
    bias = _bias_tables(rel_bias)
    gf = final_norm_g.reshape(1, D_MODEL)

    x2d = x.reshape(t, D_MODEL)
    for l in range(depth):
        g_attn = attn_norm_g[l].reshape(1, D_MODEL)
        aq, iq, kvi, iw, qm, km, vm = _proj(
            x2d, g_attn, w1[l], w2[l], q_latent_norm_g[l].reshape(1, Q_LORA),
            kv_latent_norm_g[l].reshape(1, KV_LORA), wuq[l], wukv[l], rope, bsz, seq, tm)
        ya = _dsa(aq, iq, iw, kvi, bias, bsz, seq, k_top)
        yb = _mla(qm, km, vm, tq).reshape(t, B_WIDTH)
        x2d = _merge(x2d, g_attn, wg[l], b_gate[l].reshape(1, N_BRANCHES * D_MODEL), ya, yb,
                     wa[l], wb[l], wo[l], tm)
        x2d = _mlp(x2d, mlp_norm_g[l].reshape(1, D_MODEL), wf1[l], wf2[l], gf, tm,
                   final=(l == depth - 1))
    return x2d.reshape(bsz, seq, D_MODEL)
```

```python
import functools
import math

import jax
import jax.numpy as jnp
from jax import lax
from jax.experimental import pallas as pl
from jax.experimental.pallas import tpu as pltpu

F32 = jnp.float32
BF16 = jnp.bfloat16
I32 = jnp.int32
I16 = jnp.int16

D_MODEL = 1024
A_HEADS = 8
A_HEAD_DIM = 64
A_WIDTH = A_HEADS * A_HEAD_DIM
IDX_HEADS = 4
IDX_DIM = 64
TOPK_MAX = 256
B_HEADS = 8
B_NOPE_DIM = 64
B_ROPE_DIM = 32
B_QK_DIM = B_NOPE_DIM + B_ROPE_DIM
B_V_DIM = 64
B_WIDTH = B_HEADS * B_V_DIM
Q_LORA = 256
KV_LORA = 128
ROPE_THETA = 10000.0
REL_BUCKETS = 32
REL_MAX_DIST = 128
N_BRANCHES = 2
D_FF = 4 * D_MODEL
EPS = 1e-6
NEG_INF = -1e30
LOG2E = math.log2(math.e)
IN_SIZES = (A_WIDTH, A_HEAD_DIM, A_HEAD_DIM, IDX_HEADS * IDX_DIM, IDX_DIM, IDX_HEADS,
            Q_LORA, KV_LORA, B_ROPE_DIM, N_BRANCHES * D_MODEL)

LANES = 128
SUBLANES = 8
QB = 2 * LANES
KC = QB
NPAIR = A_HEADS // 2
INT_MIN = -2 ** 31
I16_MIN = -2 ** 15
PACKED_ROWS = 2 * SUBLANES
G_IK0, G_IK1, G_KK0, G_KK1, G_VA, G_VB = range(6)
W2_COLS = 9 * LANES
VMEM_LIMIT = 56 * 1024 * 1024

NT_DIMS = (((1,), (1,)), ((), ()))


def _rms(x, g):
    return x * lax.rsqrt(jnp.mean(x * x, axis=-1, keepdims=True) + EPS) * g


def _loop_pairs(n, body):
    def pair(p, carry):
        body(2 * p)
        body(2 * p + 1)
        return carry

    lax.fori_loop(0, n // 2, pair, 0)

    @pl.when(n % 2 == 1)
    def _():
        body(n - 1)


def _proj_kernel(x_ref, g_ref, w1_ref, w2_ref, gq_ref, gkv_ref, wuq_ref, wukv_ref, rope_ref,
                 aq_ref, iq_ref, kvi_ref, iw_ref, qm_ref, km_ref, vm_ref):
    tm = x_ref.shape[0]
    hb = _rms(x_ref[...], g_ref[...]).astype(BF16)
    z1 = jnp.dot(hb, w1_ref[...], preferred_element_type=F32)
    z2 = jnp.dot(hb, w2_ref[...], preferred_element_type=F32)
    aq_ref[...] = z1[:, :A_WIDTH].astype(BF16)
    iq_ref[...] = z1[:, A_WIDTH:A_WIDTH + IDX_HEADS * IDX_DIM].astype(BF16)
    lane = lax.broadcasted_iota(I32, (tm, LANES), 1)
    for gi in range(6):
        blk = z2[:, gi * LANES:(gi + 1) * LANES]
        if gi == G_VA:
            blk = jnp.where(lane == A_HEAD_DIM, 1.0, blk)
        if gi == G_VB:
            blk = jnp.where(lane == 0, 1.0, blk)
        kvi_ref[gi] = blk.astype(BF16)
    iw_ref[...] = z2[:, 8 * LANES:9 * LANES]

    cs, s1, s2 = rope_ref[0], rope_ref[1], rope_ref[2]

    def rope(u):
        return u * cs + pltpu.roll(u, 16, 1) * s1 + pltpu.roll(u, LANES - 16, 1) * s2

    cq = z1[:, A_WIDTH + IDX_HEADS * IDX_DIM:]
    qup = jnp.dot(_rms(cq, gq_ref[...]).astype(BF16), wuq_ref[...], preferred_element_type=F32)
    scale = LOG2E * B_QK_DIM ** -0.5
    for h in range(B_HEADS):
        qm_ref[0, h] = (rope(qup[:, h * LANES:(h + 1) * LANES]) * scale).astype(BF16)
    ckv = z2[:, 6 * LANES:7 * LANES]
    kvup = jnp.dot(_rms(ckv, gkv_ref[...]).astype(BF16), wukv_ref[...],
                   preferred_element_type=F32)
    kr = rope(z2[:, 7 * LANES:8 * LANES])
    for h in range(B_HEADS):
        km_ref[0, h] = (kvup[:, h * LANES:(h + 1) * LANES] + kr).astype(BF16)
    for p in range(B_HEADS // 2):
        c0 = B_HEADS * LANES + p * LANES
        vm_ref[0, p] = kvup[:, c0:c0 + LANES].astype(BF16)


def _proj(x2d, g, w1, w2, gq, gkv, wuq, wukv, rope, bsz, seq, tm):
    t = x2d.shape[0]
    nt = seq // tm
    const = lambda i: (0, 0)
    return pl.pallas_call(
        _proj_kernel,
        grid=(t // tm,),
        in_specs=[
            pl.BlockSpec((tm, D_MODEL), lambda i: (i, 0)),
            pl.BlockSpec((1, D_MODEL), const),
            pl.BlockSpec(w1.shape, const),
            pl.BlockSpec(w2.shape, const),
            pl.BlockSpec((1, Q_LORA), const),
            pl.BlockSpec((1, KV_LORA), const),
            pl.BlockSpec(wuq.shape, const),
            pl.BlockSpec(wukv.shape, const),
            pl.BlockSpec((3, tm, LANES), lambda i: (0, i, 0)),
        ],
        out_specs=[
            pl.BlockSpec((tm, A_WIDTH), lambda i: (i, 0)),
            pl.BlockSpec((tm, IDX_HEADS * IDX_DIM), lambda i: (i, 0)),
            pl.BlockSpec((6, tm, LANES), lambda i: (0, i, 0)),
            pl.BlockSpec((tm, LANES), lambda i: (i, 0)),
            pl.BlockSpec((1, B_HEADS, tm, LANES), lambda i: (i // nt, 0, i % nt, 0)),
            pl.BlockSpec((1, B_HEADS, tm, LANES), lambda i: (i // nt, 0, i % nt, 0)),
            pl.BlockSpec((1, B_HEADS // 2, tm, LANES), lambda i: (i // nt, 0, i % nt, 0)),
        ],
        out_shape=[
            jax.ShapeDtypeStruct((t, A_WIDTH), BF16),
            jax.ShapeDtypeStruct((t, IDX_HEADS * IDX_DIM), BF16),
            jax.ShapeDtypeStruct((6, t, LANES), BF16),
            jax.ShapeDtypeStruct((t, LANES), F32),
            jax.ShapeDtypeStruct((bsz, B_HEADS, seq, LANES), BF16),
            jax.ShapeDtypeStruct((bsz, B_HEADS, seq, LANES), BF16),
            jax.ShapeDtypeStruct((bsz, B_HEADS // 2, seq, LANES), BF16),
        ],
        compiler_params=pltpu.CompilerParams(
            dimension_semantics=("arbitrary",), vmem_limit_bytes=VMEM_LIMIT),
        name="proj",
    )(x2d, g, w1, w2, gq, gkv, wuq, wukv, rope)


def _dsa_kernel(aq_ref, iq_ref, iw_ref, kvi_ref, bias_ref, out_ref,
                key_ref, hi_ref, lo_ref, tie_ref, lg_ref, mx_ref, acc_ref, *, k_top):
    i = pl.program_id(1)
    n2 = i + 1
    kf = float(k_top)
    nrow = NPAIR * QB

    row = lax.broadcasted_iota(I32, (KC, QB), 0)
    col = lax.broadcasted_iota(I32, (KC, QB), 1)

    def chunk(c):
        return pl.ds(pl.multiple_of(c * KC, KC), KC)

    iq = iq_ref[...]
    iqs = jnp.concatenate([iq[:, :LANES], iq[:, LANES:]], axis=0)
    iw_t = iw_ref[...].T
    w0, w1, w2, w3 = (iw_t[h:h + 1, :] for h in range(IDX_HEADS))

    def score_body(c, carry):
        t0 = lax.dot_general(kvi_ref[G_IK0, chunk(c), :], iqs, NT_DIMS,
                             preferred_element_type=F32)
        t1 = lax.dot_general(kvi_ref[G_IK1, chunk(c), :], iqs, NT_DIMS,
                             preferred_element_type=F32)
        s = (w0 * jnp.maximum(t0[:, :QB], 0.0) + w1 * jnp.maximum(t1[:, :QB], 0.0)
             + w2 * jnp.maximum(t0[:, QB:], 0.0) + w3 * jnp.maximum(t1[:, QB:], 0.0))
        b = lax.bitcast_convert_type(s, I32)
        sk = jnp.where(b < 0, b ^ 0x7FFFFFFF, b)
        sk = jnp.where(c * KC + row > i * QB + col, INT_MIN, sk)
        key_ref[chunk(c), :] = sk
        hi_ref[chunk(c), :] = (sk >> 16).astype(I16)
        lo_ref[chunk(c), :] = ((sk & 0xFFFF) + I16_MIN).astype(I16)
        return carry

    _loop_pairs(n2, lambda c: score_body(c, 0))

    def count16(ref, cand):
        cb = jnp.broadcast_to(cand.astype(I16), (PACKED_ROWS, QB))

        def body(c, accs):
            keys = ref[chunk(c), :]
            accs = list(accs)
            for r in range(KC // PACKED_ROWS):
                blk = keys[r * PACKED_ROWS:(r + 1) * PACKED_ROWS, :]
                accs[r % 4] = accs[r % 4] + jnp.where(blk >= cb, jnp.int16(1), jnp.int16(0))
            return tuple(accs)

        z = jnp.zeros((PACKED_ROWS, QB), I16)
        a = lax.fori_loop(0, n2, body, (z, z, z, z))
        tot = (a[0] + a[1]) + (a[2] + a[3])
        return jnp.sum(tot.astype(F32), axis=0, keepdims=True)

    def bisect16(ref):
        def bit_body(it, base):
            cand = base + lax.shift_left(jnp.int32(1), 15 - it)
            return jnp.where(count16(ref, cand) >= kf, cand, base)

        return lax.fori_loop(0, 16, bit_body, jnp.full((1, QB), I16_MIN, I32))

    tau_hi = bisect16(hi_ref)
    th = tau_hi.astype(I16)

    def refine_body(c, carry):
        hi = hi_ref[chunk(c), :]
        lo_ref[chunk(c), :] = jnp.where(
            hi == th, lo_ref[chunk(c), :],
            jnp.where(hi > th, jnp.int16(-I16_MIN - 1), jnp.int16(I16_MIN)))
        return carry

    lax.fori_loop(0, n2, refine_body, 0)
    tau = tau_hi * 65536 + (bisect16(lo_ref) - I16_MIN)

    def count_gt(cand):
        cb = jnp.broadcast_to(cand, (SUBLANES, QB))

        def body(c, accs):
            keys = key_ref[chunk(c), :]
            accs = list(accs)
            for r in range(KC // SUBLANES):
                blk = keys[r * SUBLANES:(r + 1) * SUBLANES, :]
                accs[r % 4] = accs[r % 4] + jnp.where(blk > cb, 1.0, 0.0)
            return tuple(accs)

        z = jnp.zeros((SUBLANES, QB), F32)
        a = lax.fori_loop(0, n2, body, (z, z, z, z))
        return jnp.sum((a[0] + a[1]) + (a[2] + a[3]), axis=0, keepdims=True)

    need = kf - count_gt(tau)

    ltri = jnp.where(row >= col, 1.0, 0.0).astype(BF16)

    def mask_chunk(c):
        blk = key_ref[chunk(c), :]
        eq = blk == tau
        pref = jnp.dot(ltri, jnp.where(eq, 1.0, 0.0).astype(BF16),
                       preferred_element_type=F32)
        carry = tie_ref[...]
        sel = (blk > tau) | (eq & (pref + carry <= need))
        sel = sel & (c * KC + row <= i * QB + col)
        tie_ref[...] = carry + pref[KC - 1:KC, :]
        return jnp.where(sel, jnp.inf, NEG_INF).T

    tie_ref[...] = jnp.zeros(tie_ref.shape, F32)

    aq = aq_ref[...]
    qs = jnp.concatenate([aq[:, j * LANES:(j + 1) * LANES] for j in range(NPAIR)], axis=0)
    mx_ref[...] = jnp.full(mx_ref.shape, -jnp.inf, F32)

    def logits_chunk(c2, bias_pos):
        mc = mask_chunk(c2)
        for r, g in ((0, G_KK0), (1, G_KK1)):
            lg = lax.dot_general(qs, kvi_ref[g, chunk(c2), :], NT_DIMS,
                                 preferred_element_type=F32)
            if bias_pos is not None:
                lg = lg + bias_ref[bias_pos, r]
            for j in range(NPAIR):
                rows = slice(j * QB, (j + 1) * QB)
                lj = jnp.minimum(lg[rows], mc)
                lg_ref[r, c2, rows, :] = lj
                mrows = slice(r * nrow + j * QB, r * nrow + (j + 1) * QB)
                mx_ref[mrows, :] = jnp.maximum(mx_ref[mrows, :],
                                               jnp.maximum(lj[:, :LANES], lj[:, LANES:]))

    _loop_pairs(jnp.maximum(n2 - 2, 0), lambda c2: logits_chunk(c2, None))

    @pl.when(n2 >= 2)
    def _():
        logits_chunk(n2 - 2, 0)

    logits_chunk(n2 - 1, 1)

    mrow = jnp.max(mx_ref[...], axis=1, keepdims=True)
    acc_ref[...] = jnp.zeros(acc_ref.shape, F32)

    def pv_body(c2, carry):
        for r, g in ((0, G_VA), (1, G_VB)):
            p = jnp.exp2(lg_ref[r, c2] - mrow[r * nrow:(r + 1) * nrow])
            acc_ref[r * nrow:(r + 1) * nrow, :] += jnp.dot(
                p.astype(BF16), kvi_ref[g, chunk(c2), :], preferred_element_type=F32)
        return carry

    _loop_pairs(n2, lambda c2: pv_body(c2, 0))

    lane = lax.broadcasted_iota(I32, (QB, LANES), 1)
    for j in range(NPAIR):
        oe = acc_ref[j * QB:(j + 1) * QB, :]
        oo = acc_ref[nrow + j * QB:nrow + (j + 1) * QB, :]
        out = jnp.where(lane < A_HEAD_DIM, oe / oe[:, A_HEAD_DIM:A_HEAD_DIM + 1], oo / oo[:, 0:1])
        out_ref[:, j * LANES:(j + 1) * LANES] = out.astype(BF16)


def _dsa(aq, iq, iw, kvi, bias, bsz, seq, k_top):
    t = aq.shape[0]
    nq = seq // QB
    return pl.pallas_call(
        functools.partial(_dsa_kernel, k_top=k_top),
        grid=(bsz, nq),
        in_specs=[
            pl.BlockSpec((QB, A_WIDTH), lambda b, i: (b * nq + i, 0)),
            pl.BlockSpec((QB, IDX_HEADS * IDX_DIM), lambda b, i: (b * nq + i, 0)),
            pl.BlockSpec((QB, LANES), lambda b, i: (b * nq + i, 0)),
            pl.BlockSpec((6, seq, LANES), lambda b, i: (0, b, 0)),
            pl.BlockSpec(bias.shape, lambda b, i: (0, 0, 0, 0)),
        ],
        out_specs=pl.BlockSpec((QB, A_WIDTH), lambda b, i: (b * nq + i, 0)),
        out_shape=jax.ShapeDtypeStruct((t, A_WIDTH), BF16),
        scratch_shapes=[
            pltpu.VMEM((seq, QB), I32),
            pltpu.VMEM((seq, QB), I16),
            pltpu.VMEM((seq, QB), I16),
            pltpu.VMEM((1, QB), F32),
            pltpu.VMEM((2, nq, NPAIR * QB, KC), F32),
            pltpu.VMEM((A_HEADS * QB, LANES), F32),
            pltpu.VMEM((A_HEADS * QB, LANES), F32),
        ],
        compiler_params=pltpu.CompilerParams(
            dimension_semantics=("arbitrary", "arbitrary"), vmem_limit_bytes=VMEM_LIMIT),
        name="dsa",
    )(aq, iq, iw, kvi, bias)


def _mla_kernel(q_ref, k_ref, v_ref, out_ref, lg_ref, mx_ref, ls_ref, acc_ref):
    tq = q_ref.shape[2]
    i = pl.program_id(2)
    lane = lax.broadcasted_iota(I32, (tq, LANES), 1)

    def chunk(c):
        return pl.ds(pl.multiple_of(c * tq, tq), tq)

    def lane_fold(x, op):
        part = x[:, :LANES]
        for s in range(1, tq // LANES):
            part = op(part, x[:, s * LANES:(s + 1) * LANES])
        return part

    mx_ref[...] = jnp.full(mx_ref.shape, -jnp.inf, F32)

    def logits(c, diag):
        for e in range(2):
            lg = lax.dot_general(q_ref[0, e], k_ref[0, e, chunk(c), :], NT_DIMS,
                                 preferred_element_type=F32)
            if diag:
                row = lax.broadcasted_iota(I32, (tq, tq), 0)
                col = lax.broadcasted_iota(I32, (tq, tq), 1)
                lg = jnp.where(col <= row, lg, NEG_INF)
            lg_ref[e, c] = lg
            mx_ref[e] = jnp.maximum(mx_ref[e], lane_fold(lg, jnp.maximum))

    _loop_pairs(i, lambda c: logits(c, False))
    logits(i, True)
    m = [jnp.max(mx_ref[e], axis=1, keepdims=True) for e in range(2)]
    ls_ref[...] = jnp.zeros(ls_ref.shape, F32)
    acc_ref[...] = jnp.zeros(acc_ref.shape, F32)

    def pv(c):
        v = v_ref[0, 0, chunk(c), :]
        for e in range(2):
            p = jnp.exp2(lg_ref[e, c] - m[e])
            ls_ref[e] += lane_fold(p, jnp.add)
            acc_ref[e] += jnp.dot(p.astype(BF16), v, preferred_element_type=F32)

    _loop_pairs(i + 1, pv)
    outs = [acc_ref[e] / jnp.sum(ls_ref[e], axis=1, keepdims=True) for e in range(2)]
    out_ref[0] = jnp.where(lane < B_V_DIM, outs[0], outs[1]).astype(BF16)


def _mla(qm, km, vm, tq):
    bsz, _, seq, _ = qm.shape
    return pl.pallas_call(
        _mla_kernel,
        grid=(bsz, B_HEADS // 2, seq // tq),
        in_specs=[
            pl.BlockSpec((1, 2, tq, LANES), lambda b, p, i: (b, p, i, 0)),
            pl.BlockSpec((1, 2, seq, LANES), lambda b, p, i: (b, p, 0, 0)),
            pl.BlockSpec((1, 1, seq, LANES), lambda b, p, i: (b, p, 0, 0)),
        ],
        out_specs=pl.BlockSpec((1, tq, LANES), lambda b, p, i: (b, i, p)),
        out_shape=jax.ShapeDtypeStruct((bsz, seq, B_WIDTH), BF16),
        scratch_shapes=[
            pltpu.VMEM((2, seq // tq, tq, tq), F32),
            pltpu.VMEM((2, tq, LANES), F32),
            pltpu.VMEM((2, tq, LANES), F32),
            pltpu.VMEM((2, tq, LANES), F32),
        ],
        compiler_params=pltpu.CompilerParams(
            dimension_semantics=("arbitrary", "arbitrary", "arbitrary"),
            vmem_limit_bytes=VMEM_LIMIT),
        name="mla",
    )(qm, km, vm)


def _merge_kernel(x_ref, g_ref, wg_ref, bg_ref, ya_ref, yb_ref, wa_ref, wb_ref, wo_ref, o_ref):
    x = x_ref[...]
    hb = _rms(x, g_ref[...]).astype(BF16)
    gl = jnp.dot(hb, wg_ref[...], preferred_element_type=F32) + bg_ref[...]
    gates = 1.0 / (1.0 + jnp.exp(-gl))
    pa = jnp.dot(ya_ref[...], wa_ref[...], preferred_element_type=F32)
    pb = jnp.dot(yb_ref[...], wb_ref[...], preferred_element_type=F32)
    merged = gates[:, :D_MODEL] * pa + gates[:, D_MODEL:] * pb
    o_ref[...] = x + jnp.dot(merged.astype(BF16), wo_ref[...], preferred_element_type=F32)


def _merge(x2d, g, wg, bg, ya, yb, wa, wb, wo, tm):
    t = x2d.shape[0]
    const = lambda i: (0, 0)
    return pl.pallas_call(
        _merge_kernel,
        grid=(t // tm,),
        in_specs=[
            pl.BlockSpec((tm, D_MODEL), lambda i: (i, 0)),
            pl.BlockSpec((1, D_MODEL), const),
            pl.BlockSpec(wg.shape, const),
            pl.BlockSpec((1, N_BRANCHES * D_MODEL), const),
            pl.BlockSpec((tm, A_WIDTH), lambda i: (i, 0)),
            pl.BlockSpec((tm, B_WIDTH), lambda i: (i, 0)),
            pl.BlockSpec(wa.shape, const),
            pl.BlockSpec(wb.shape, const),
            pl.BlockSpec(wo.shape, const),
        ],
        out_specs=pl.BlockSpec((tm, D_MODEL), lambda i: (i, 0)),
        out_shape=jax.ShapeDtypeStruct((t, D_MODEL), F32),
        compiler_params=pltpu.CompilerParams(
            dimension_semantics=("arbitrary",), vmem_limit_bytes=VMEM_LIMIT),
        name="merge",
    )(x2d, g, wg, bg, ya, yb, wa, wb, wo)


def _mlp_kernel(x_ref, g_ref, w1_ref, w2_ref, gf_ref, o_ref, *, final, fc):
    x = x_ref[...]
    hb = _rms(x, g_ref[...]).astype(BF16)
    acc = jnp.zeros(x.shape, F32)
    for j in range(D_FF // fc):
        a = jnp.dot(hb, w1_ref[:, j * fc:(j + 1) * fc], preferred_element_type=F32)
        a = jnp.square(jnp.maximum(a, 0.0)).astype(BF16)
        acc = acc + jnp.dot(a, w2_ref[j * fc:(j + 1) * fc, :], preferred_element_type=F32)
    y = x + acc
    if final:
        y = _rms(y, gf_ref[...])
    o_ref[...] = y


def _mlp(x2d, g, w1, w2, gf, tm, final):
    t = x2d.shape[0]
    const = lambda i: (0, 0)
    return pl.pallas_call(
        functools.partial(_mlp_kernel, final=final, fc=1024),
        grid=(t // tm,),
        in_specs=[
            pl.BlockSpec((tm, D_MODEL), lambda i: (i, 0)),
            pl.BlockSpec((1, D_MODEL), const),
            pl.BlockSpec(w1.shape, const, pipeline_mode=pl.Buffered(1)),
            pl.BlockSpec(w2.shape, const, pipeline_mode=pl.Buffered(1)),
            pl.BlockSpec((1, D_MODEL), const),
        ],
        out_specs=pl.BlockSpec((tm, D_MODEL), lambda i: (i, 0)),
        out_shape=jax.ShapeDtypeStruct((t, D_MODEL), F32),
        compiler_params=pltpu.CompilerParams(
            dimension_semantics=("arbitrary",), vmem_limit_bytes=VMEM_LIMIT),
        name="mlp",
    )(x2d, g, w1, w2, gf)


def _split_w_in(w_in):
    cuts = [0]
    for s in IN_SIZES:
        cuts.append(cuts[-1] + s)
    return [w_in[..., cuts[n]:cuts[n + 1]] for n in range(len(IN_SIZES))]


def _prep_weights(w_in, w_uq, w_ukv):
    depth = w_in.shape[0]
    a_q, a_k, a_v, i_q, i_k, i_w, c_q, c_kv, k_r, w_g = _split_w_in(w_in)
    z64 = jnp.zeros((depth, D_MODEL, 64), F32)
    w1 = jnp.concatenate([a_q * (LOG2E * A_HEAD_DIM ** -0.5), i_q * IDX_DIM ** -0.5, c_q], axis=-1)
    w2 = jnp.concatenate([
        i_k, z64, z64, i_k, a_k, z64, z64, a_k, a_v, z64, z64, a_v, c_kv,
        z64, k_r, jnp.zeros((depth, D_MODEL, 32), F32),
        i_w * IDX_HEADS ** -0.5, jnp.zeros((depth, D_MODEL, LANES - IDX_HEADS), F32)], axis=-1)
    assert w2.shape[-1] == W2_COLS
    wuq = w_uq.reshape(depth, Q_LORA, B_HEADS, B_QK_DIM)
    wuq = jnp.pad(wuq, ((0, 0), (0, 0), (0, 0), (0, LANES - B_QK_DIM)))
    wuq = wuq.reshape(depth, Q_LORA, B_HEADS * LANES)
    wukv = w_ukv.reshape(depth, KV_LORA, B_HEADS, B_NOPE_DIM + B_V_DIM)
    wk = jnp.pad(wukv[..., :B_NOPE_DIM], ((0, 0), (0, 0), (0, 0), (0, LANES - B_NOPE_DIM)))
    wk = wk.reshape(depth, KV_LORA, B_HEADS * LANES)
    wv = wukv[..., B_NOPE_DIM:].reshape(depth, KV_LORA, B_HEADS * B_V_DIM)
    wukv = jnp.concatenate([wk, wv], axis=-1)
    return (w1.astype(BF16), w2.astype(BF16), w_g.astype(BF16), wuq.astype(BF16),
            wukv.astype(BF16))


def _rope_tables(positions):
    inv_freq = ROPE_THETA ** (-jnp.arange(0, B_ROPE_DIM, 2, dtype=F32) / B_ROPE_DIM)
    ang = positions.astype(F32).reshape(-1)[:, None] * inv_freq
    cos, sin = jnp.cos(ang), jnp.sin(ang)
    t = ang.shape[0]
    half = B_ROPE_DIM // 2
    one = jnp.ones((t, B_NOPE_DIM), F32)
    z = lambda n: jnp.zeros((t, n), F32)
    cs = jnp.concatenate([one, cos, cos, z(LANES - B_QK_DIM)], axis=1)
    s1 = jnp.concatenate([z(B_NOPE_DIM + half), sin, z(LANES - B_QK_DIM)], axis=1)
    s2 = jnp.concatenate([z(B_NOPE_DIM), -sin, z(half + LANES - B_QK_DIM)], axis=1)
    return jnp.stack([cs, s1, s2])


def _rel_bucket(dist):
    max_exact = REL_BUCKETS // 2
    n = jnp.maximum(dist, 0)
    nf = jnp.maximum(n.astype(F32), 1.0)
    log_b = max_exact + (jnp.log(nf / max_exact) / math.log(REL_MAX_DIST / max_exact)
                         * (REL_BUCKETS - max_exact)).astype(I32)
    return jnp.where(n < max_exact, n, jnp.minimum(log_b, REL_BUCKETS - 1))


def _bias_tables(rel_bias):
    rel = rel_bias - rel_bias[REL_BUCKETS - 1][None, :]
    q = jnp.arange(QB, dtype=I32)[:, None]
    s = jnp.arange(KC, dtype=I32)[None, :]

    def lookup(dist):
        bucket = _rel_bucket(dist)[..., None]
        out = jnp.zeros((QB, KC, A_HEADS), F32)
        for b in range(REL_BUCKETS - 1):
            out = jnp.where(bucket == b, rel[b][None, None, :], out)
        return out

    diag = jnp.where((q >= s)[..., None], lookup(q - s), 0.0)
    prev = lookup(KC + q - s)
    table = jnp.stack([prev, diag])
    table = table.reshape(2, QB, KC, NPAIR, 2)
    table = jnp.transpose(table, (0, 4, 3, 1, 2))
    return (table.reshape(2, 2, NPAIR * QB, KC) * LOG2E).astype(F32)


def kernel(x, positions, attn_norm_g, w_in, b_gate, q_latent_norm_g, kv_latent_norm_g, w_uq,
           w_ukv, w_branch_a, w_branch_b, w_out, mlp_norm_g, w_ff1, w_ff2, rel_bias,
           final_norm_g):
    bsz, seq, _ = x.shape
    depth = w_in.shape[0]
    t = bsz * seq
    k_top = min(TOPK_MAX, seq // 4)
    tm = min(512, seq)
    tq = min(512, seq)

    w1, w2, wg, wuq, wukv = _prep_weights(w_in, w_uq, w_ukv)
    wa, wb, wo = w_branch_a.astype(BF16), w_branch_b.astype(BF16), w_out.astype(BF16)
    wf1, wf2 = w_ff1.astype(BF16), w_ff2.astype(BF16)
    rope = _rope_tables(positions)
    bias = _bias_tables(rel_bias)
    gf = final_norm_g.reshape(1, D_MODEL)

    x2d = x.reshape(t, D_MODEL)
    for l in range(depth):
        g_attn = attn_norm_g[l].reshape(1, D_MODEL)
        aq, iq, kvi, iw, qm, km, vm = _proj(
            x2d, g_attn, w1[l], w2[l], q_latent_norm_g[l].reshape(1, Q_LORA),
            kv_latent_norm_g[l].reshape(1, KV_LORA), wuq[l], wukv[l], rope, bsz, seq, tm)
        ya = _dsa(aq, iq, iw, kvi, bias, bsz, seq, k_top)
        yb = _mla(qm, km, vm, tq).reshape(t, B_WIDTH)
        x2d = _merge(x2d, g_attn, wg[l], b_gate[l].reshape(1, N_BRANCHES * D_MODEL), ya, yb,
                     wa[l], wb[l], wo[l], tm)
        x2d = _mlp(x2d, mlp_norm_g[l].reshape(1, D_MODEL), wf1[l], wf2[l], gf, tm,
                   final=(l == depth - 1))
    return x2d.reshape(bsz, seq, D_MODEL)
```

```python
import functools
import math

import jax
import jax.numpy as jnp
from jax import lax
from jax.experimental import pallas as pl
from jax.experimental.pallas import tpu as pltpu

F32 = jnp.float32
BF16 = jnp.bfloat16
I32 = jnp.int32
I16 = jnp.int16

D_MODEL = 1024
A_HEADS = 8
A_HEAD_DIM = 64
A_WIDTH = A_HEADS * A_HEAD_DIM
IDX_HEADS = 4
IDX_DIM = 64
TOPK_MAX = 256
B_HEADS = 8
B_NOPE_DIM = 64
B_ROPE_DIM = 32
B_QK_DIM = B_NOPE_DIM + B_ROPE_DIM
B_V_DIM = 64
B_WIDTH = B_HEADS * B_V_DIM
Q_LORA = 256
KV_LORA = 128
ROPE_THETA = 10000.0
REL_BUCKETS = 32
REL_MAX_DIST = 128
N_BRANCHES = 2
D_FF = 4 * D_MODEL
EPS = 1e-6
NEG_INF = -1e30
LOG2E = math.log2(math.e)
IN_SIZES = (A_WIDTH, A_HEAD_DIM, A_HEAD_DIM, IDX_HEADS * IDX_DIM, IDX_DIM, IDX_HEADS,
            Q_LORA, KV_LORA, B_ROPE_DIM, N_BRANCHES * D_MODEL)

LANES = 128
SUBLANES = 8
QB = 2 * LANES
KC = QB
NPAIR = A_HEADS // 2
INT_MIN = -2 ** 31
I16_MIN = -2 ** 15
PACKED_ROWS = 2 * SUBLANES
G_IK0, G_IK1, G_KK0, G_KK1, G_VA, G_VB = range(6)
W2_COLS = 9 * LANES
VMEM_LIMIT = 56 * 1024 * 1024

NT_DIMS = (((1,), (1,)), ((), ()))


def _rms(x, g):
    return x * lax.rsqrt(jnp.mean(x * x, axis=-1, keepdims=True) + EPS) * g


def _loop_pairs(n, body):
    def pair(p, carry):
        body(2 * p)
        body(2 * p + 1)
        return carry

    lax.fori_loop(0, n // 2, pair, 0)

    @pl.when(n % 2 == 1)
    def _():
        body(n - 1)


def _proj_kernel(x_ref, g_ref, w1_ref, w2_ref, gq_ref, gkv_ref, wuq_ref, wukv_ref, rope_ref,
                 aq_ref, iq_ref, kvi_ref, iw_ref, qm_ref, km_ref, vm_ref):
    tm = x_ref.shape[0]
    hb = _rms(x_ref[...], g_ref[...]).astype(BF16)
    z1 = jnp.dot(hb, w1_ref[...], preferred_element_type=F32)
    z2 = jnp.dot(hb, w2_ref[...], preferred_element_type=F32)
    aq_ref[...] = z1[:, :A_WIDTH].astype(BF16)
    iq_ref[...] = z1[:, A_WIDTH:A_WIDTH + IDX_HEADS * IDX_DIM].astype(BF16)
    lane = lax.broadcasted_iota(I32, (tm, LANES), 1)
    for gi in range(6):
        blk = z2[:, gi * LANES:(gi + 1) * LANES]
        if gi == G_VA:
            blk = jnp.where(lane == A_HEAD_DIM, 1.0, blk)
        if gi == G_VB:
            blk = jnp.where(lane == 0, 1.0, blk)
        kvi_ref[gi] = blk.astype(BF16)
    iw_ref[...] = z2[:, 8 * LANES:9 * LANES]

    cs, s1, s2 = rope_ref[0], rope_ref[1], rope_ref[2]

    def rope(u):
        return u * cs + pltpu.roll(u, 16, 1) * s1 + pltpu.roll(u, LANES - 16, 1) * s2

    cq = z1[:, A_WIDTH + IDX_HEADS * IDX_DIM:]
    qup = jnp.dot(_rms(cq, gq_ref[...]).astype(BF16), wuq_ref[...], preferred_element_type=F32)
    scale = LOG2E * B_QK_DIM ** -0.5
    for h in range(B_HEADS):
        qm_ref[0, h] = (rope(qup[:, h * LANES:(h + 1) * LANES]) * scale).astype(BF16)
    ckv = z2[:, 6 * LANES:7 * LANES]
    kvup = jnp.dot(_rms(ckv, gkv_ref[...]).astype(BF16), wukv_ref[...],
                   preferred_element_type=F32)
    kr = rope(z2[:, 7 * LANES:8 * LANES])
    for h in range(B_HEADS):
        km_ref[0, h] = (kvup[:, h * LANES:(h + 1) * LANES] + kr).astype(BF16)
    for p in range(B_HEADS // 2):
        c0 = B_HEADS * LANES + p * LANES
        vm_ref[0, p] = kvup[:, c0:c0 + LANES].astype(BF16)


def _proj(x2d, g, w1, w2, gq, gkv, wuq, wukv, rope, bsz, seq, tm):
    t = x2d.shape[0]
    nt = seq // tm
    const = lambda i: (0, 0)
    return pl.pallas_call(
        _proj_kernel,
        grid=(t // tm,),
        in_specs=[
            pl.BlockSpec((tm, D_MODEL), lambda i: (i, 0)),
            pl.BlockSpec((1, D_MODEL), const),
            pl.BlockSpec(w1.shape, const),
            pl.BlockSpec(w2.shape, const),
            pl.BlockSpec((1, Q_LORA), const),
            pl.BlockSpec((1, KV_LORA), const),
            pl.BlockSpec(wuq.shape, const),
            pl.BlockSpec(wukv.shape, const),
            pl.BlockSpec((3, tm, LANES), lambda i: (0, i, 0)),
        ],
        out_specs=[
            pl.BlockSpec((tm, A_WIDTH), lambda i: (i, 0)),
            pl.BlockSpec((tm, IDX_HEADS * IDX_DIM), lambda i: (i, 0)),
            pl.BlockSpec((6, tm, LANES), lambda i: (0, i, 0)),
            pl.BlockSpec((tm, LANES), lambda i: (i, 0)),
            pl.BlockSpec((1, B_HEADS, tm, LANES), lambda i: (i // nt, 0, i % nt, 0)),
            pl.BlockSpec((1, B_HEADS, tm, LANES), lambda i: (i // nt, 0, i % nt, 0)),
            pl.BlockSpec((1, B_HEADS // 2, tm, LANES), lambda i: (i // nt, 0, i % nt, 0)),
        ],
        out_shape=[
            jax.ShapeDtypeStruct((t, A_WIDTH), BF16),
            jax.ShapeDtypeStruct((t, IDX_HEADS * IDX_DIM), BF16),
            jax.ShapeDtypeStruct((6, t, LANES), BF16),
            jax.ShapeDtypeStruct((t, LANES), F32),
            jax.ShapeDtypeStruct((bsz, B_HEADS, seq, LANES), BF16),
            jax.ShapeDtypeStruct((bsz, B_HEADS, seq, LANES), BF16),
            jax.ShapeDtypeStruct((bsz, B_HEADS // 2, seq, LANES), BF16),
        ],
        compiler_params=pltpu.CompilerParams(
            dimension_semantics=("arbitrary",), vmem_limit_bytes=VMEM_LIMIT),
        name="proj",
    )(x2d, g, w1, w2, gq, gkv, wuq, wukv, rope)


def _dsa_kernel(aq_ref, iq_ref, iw_ref, kvi_ref, bias_ref, out_ref,
                key_ref, hi_ref, lo_ref, tie_ref, lg_ref, mx_ref, acc_ref, *, k_top):
    i = pl.program_id(1)
    n2 = i + 1
    kf = float(k_top)
    nrow = NPAIR * QB

    row = lax.broadcasted_iota(I32, (KC, QB), 0)
    col = lax.broadcasted_iota(I32, (KC, QB), 1)

    def chunk(c):
        return pl.ds(pl.multiple_of(c * KC, KC), KC)

    iq = iq_ref[...]
    iqs = jnp.concatenate([iq[:, :LANES], iq[:, LANES:]], axis=0)
    iw_t = iw_ref[...].T
    w0, w1, w2, w3 = (iw_t[h:h + 1, :] for h in range(IDX_HEADS))

    def score_body(c, carry):
        t0 = lax.dot_general(kvi_ref[G_IK0, chunk(c), :], iqs, NT_DIMS,
                             preferred_element_type=F32)
        t1 = lax.dot_general(kvi_ref[G_IK1, chunk(c), :], iqs, NT_DIMS,
                             preferred_element_type=F32)
        s = (w0 * jnp.maximum(t0[:, :QB], 0.0) + w1 * jnp.maximum(t1[:, :QB], 0.0)
             + w2 * jnp.maximum(t0[:, QB:], 0.0) + w3 * jnp.maximum(t1[:, QB:], 0.0))
        b = lax.bitcast_convert_type(s, I32)
        sk = jnp.where(b < 0, b ^ 0x7FFFFFFF, b)
        sk = jnp.where(c * KC + row > i * QB + col, INT_MIN, sk)
        key_ref[chunk(c), :] = sk
        hi_ref[chunk(c), :] = (sk >> 16).astype(I16)
        lo_ref[chunk(c), :] = ((sk & 0xFFFF) + I16_MIN).astype(I16)
        return carry

    _loop_pairs(n2, lambda c: score_body(c, 0))

    def count16(ref, cand):
        cb = jnp.broadcast_to(cand.astype(I16), (PACKED_ROWS, QB))

        def body(c, accs):
            keys = ref[chunk(c), :]
            accs = list(accs)
            for r in range(KC // PACKED_ROWS):
                blk = keys[r * PACKED_ROWS:(r + 1) * PACKED_ROWS, :]
                accs[r % 4] = accs[r % 4] + jnp.where(blk >= cb, jnp.int16(1), jnp.int16(0))
            return tuple(accs)

        z = jnp.zeros((PACKED_ROWS, QB), I16)
        a = lax.fori_loop(0, n2, body, (z, z, z, z))
        tot = (a[0] + a[1]) + (a[2] + a[3])
        return jnp.sum(tot.astype(F32), axis=0, keepdims=True)

    def bisect16(ref):
        def bit_body(it, base):
            cand = base + lax.shift_left(jnp.int32(1), 15 - it)
            return jnp.where(count16(ref, cand) >= kf, cand, base)

        return lax.fori_loop(0, 16, bit_body, jnp.full((1, QB), I16_MIN, I32))

    tau_hi = bisect16(hi_ref)
    th = tau_hi.astype(I16)

    def refine_body(c, carry):
        hi = hi_ref[chunk(c), :]
        lo_ref[chunk(c), :] = jnp.where(
            hi == th, lo_ref[chunk(c), :],
            jnp.where(hi > th, jnp.int16(-I16_MIN - 1), jnp.int16(I16_MIN)))
        return carry

    lax.fori_loop(0, n2, refine_body, 0)
    tau = tau_hi * 65536 + (bisect16(lo_ref) - I16_MIN)

    def count_gt(cand):
        cb = jnp.broadcast_to(cand, (SUBLANES, QB))

        def body(c, accs):
            keys = key_ref[chunk(c), :]
            accs = list(accs)
            for r in range(KC // SUBLANES):
                blk = keys[r * SUBLANES:(r + 1) * SUBLANES, :]
                accs[r % 4] = accs[r % 4] + jnp.where(blk > cb, 1.0, 0.0)
            return tuple(accs)

        z = jnp.zeros((SUBLANES, QB), F32)
        a = lax.fori_loop(0, n2, body, (z, z, z, z))
        return jnp.sum((a[0] + a[1]) + (a[2] + a[3]), axis=0, keepdims=True)

    need = kf - count_gt(tau)

    ltri = jnp.where(row >= col, 1.0, 0.0).astype(BF16)

    def mask_chunk(c):
        blk = key_ref[chunk(c), :]
        eq = blk == tau
        pref = jnp.dot(ltri, jnp.where(eq, 1.0, 0.0).astype(BF16),
                       preferred_element_type=F32)
        carry = tie_ref[...]
        sel = (blk > tau) | (eq & (pref + carry <= need))
        sel = sel & (c * KC + row <= i * QB + col)
        tie_ref[...] = carry + pref[KC - 1:KC, :]
        return jnp.where(sel, jnp.inf, NEG_INF).T

    tie_ref[...] = jnp.zeros(tie_ref.shape, F32)

    aq = aq_ref[...]
    qs = jnp.concatenate([aq[:, j * LANES:(j + 1) * LANES] for j in range(NPAIR)], axis=0)
    mx_ref[...] = jnp.full(mx_ref.shape, -jnp.inf, F32)

    def logits_chunk(c2, bias_pos):
        mc = mask_chunk(c2)
        for r, g in ((0, G_KK0), (1, G_KK1)):
            lg = lax.dot_general(qs, kvi_ref[g, chunk(c2), :], NT_DIMS,
                                 preferred_element_type=F32)
            if bias_pos is not None:
                lg = lg + bias_ref[bias_pos, r]
            for j in range(NPAIR):
                rows = slice(j * QB, (j + 1) * QB)
                lj = jnp.minimum(lg[rows], mc)
                lg_ref[r, c2, rows, :] = lj
                mrows = slice(r * nrow + j * QB, r * nrow + (j + 1) * QB)
                mx_ref[mrows, :] = jnp.maximum(mx_ref[mrows, :],
                                               jnp.maximum(lj[:, :LANES], lj[:, LANES:]))

    _loop_pairs(jnp.maximum(n2 - 2, 0), lambda c2: logits_chunk(c2, None))

    @pl.when(n2 >= 2)
    def _():
        logits_chunk(n2 - 2, 0)

    logits_chunk(n2 - 1, 1)

    mrow = jnp.max(mx_ref[...], axis=1, keepdims=True)
    acc_ref[...] = jnp.zeros(acc_ref.shape, F32)

    def pv_body(c2, carry):
        for r, g in ((0, G_VA), (1, G_VB)):
            p = jnp.exp2(lg_ref[r, c2] - mrow[r * nrow:(r + 1) * nrow])
            acc_ref[r * nrow:(r + 1) * nrow, :] += jnp.dot(
                p.astype(BF16), kvi_ref[g, chunk(c2), :], preferred_element_type=F32)
        return carry

    _loop_pairs(n2, lambda c2: pv_body(c2, 0))

    lane = lax.broadcasted_iota(I32, (QB, LANES), 1)
    for j in range(NPAIR):
        oe = acc_ref[j * QB:(j + 1) * QB, :]
        oo = acc_ref[nrow + j * QB:nrow + (j + 1) * QB, :]
        out = jnp.where(lane < A_HEAD_DIM, oe / oe[:, A_HEAD_DIM:A_HEAD_DIM + 1], oo / oo[:, 0:1])
        out_ref[:, j * LANES:(j + 1) * LANES] = out.astype(BF16)


def _dsa(aq, iq, iw, kvi, bias, bsz, seq, k_top):
    t = aq.shape[0]
    nq = seq // QB
    return pl.pallas_call(
        functools.partial(_dsa_kernel, k_top=k_top),
        grid=(bsz, nq),
        in_specs=[
            pl.BlockSpec((QB, A_WIDTH), lambda b, i: (b * nq + i, 0)),
            pl.BlockSpec((QB, IDX_HEADS * IDX_DIM), lambda b, i: (b * nq + i, 0)),
            pl.BlockSpec((QB, LANES), lambda b, i: (b * nq + i, 0)),
            pl.BlockSpec((6, seq, LANES), lambda b, i: (0, b, 0)),
            pl.BlockSpec(bias.shape, lambda b, i: (0, 0, 0, 0)),
        ],
        out_specs=pl.BlockSpec((QB, A_WIDTH), lambda b, i: (b * nq + i, 0)),
        out_shape=jax.ShapeDtypeStruct((t, A_WIDTH), BF16),
        scratch_shapes=[
            pltpu.VMEM((seq, QB), I32),
            pltpu.VMEM((seq, QB), I16),
            pltpu.VMEM((seq, QB), I16),
            pltpu.VMEM((1, QB), F32),
            pltpu.VMEM((2, nq, NPAIR * QB, KC), F32),
            pltpu.VMEM((A_HEADS * QB, LANES), F32),
            pltpu.VMEM((A_HEADS * QB, LANES), F32),
        ],
        compiler_params=pltpu.CompilerParams(
            dimension_semantics=("arbitrary", "arbitrary"), vmem_limit_bytes=VMEM_LIMIT),
        name="dsa",
    )(aq, iq, iw, kvi, bias)


def _mla_kernel(q_ref, k_ref, v_ref, out_ref, lg_ref, mx_ref, ls_ref, acc_ref):
    tq = q_ref.shape[2]
    lane = lax.broadcasted_iota(I32, (tq, LANES), 1)

    def lane_fold(x, op):
        part = x[:, :LANES]
        for s in range(1, tq // LANES):
            part = op(part, x[:, s * LANES:(s + 1) * LANES])
        return part

    def logits(c, diag):
        for e in range(2):
            lg = lax.dot_general(q_ref[0, e], k_ref[0, e, c * tq:(c + 1) * tq, :], NT_DIMS,
                                 preferred_element_type=F32)
            if diag:
                row = lax.broadcasted_iota(I32, (tq, tq), 0)
                col = lax.broadcasted_iota(I32, (tq, tq), 1)
                lg = jnp.where(col <= row, lg, NEG_INF)
            lg_ref[e, c] = lg
            part = lane_fold(lg, jnp.maximum)
            mx_ref[e] = part if c == 0 else jnp.maximum(mx_ref[e], part)

    def pv(c, m):
        v = v_ref[0, 0, c * tq:(c + 1) * tq, :]
        for e in range(2):
            p = jnp.exp2(lg_ref[e, c] - m[e])
            part = lane_fold(p, jnp.add)
            ls_ref[e] = part if c == 0 else ls_ref[e] + part
            o = jnp.dot(p.astype(BF16), v, preferred_element_type=F32)
            acc_ref[e] = o if c == 0 else acc_ref[e] + o

    def attend(n):
        for c in range(n):
            logits(c, False)
        logits(n, True)
        m = [jnp.max(mx_ref[e], axis=1, keepdims=True) for e in range(2)]
        for c in range(n + 1):
            pv(c, m)
        outs = [acc_ref[e] / jnp.sum(ls_ref[e], axis=1, keepdims=True) for e in range(2)]
        out_ref[0] = jnp.where(lane < B_V_DIM, outs[0], outs[1]).astype(BF16)

    for n in range(k_ref.shape[2] // tq):
        pl.when(pl.program_id(2) == n)(functools.partial(attend, n))


def _mla(qm, km, vm, tq):
    bsz, _, seq, _ = qm.shape
    return pl.pallas_call(
        _mla_kernel,
        grid=(bsz, B_HEADS // 2, seq // tq),
        in_specs=[
            pl.BlockSpec((1, 2, tq, LANES), lambda b, p, i: (b, p, i, 0)),
            pl.BlockSpec((1, 2, seq, LANES), lambda b, p, i: (b, p, 0, 0)),
            pl.BlockSpec((1, 1, seq, LANES), lambda b, p, i: (b, p, 0, 0)),
        ],
        out_specs=pl.BlockSpec((1, tq, LANES), lambda b, p, i: (b, i, p)),
        out_shape=jax.ShapeDtypeStruct((bsz, seq, B_WIDTH), BF16),
        scratch_shapes=[
            pltpu.VMEM((2, seq // tq, tq, tq), F32),
            pltpu.VMEM((2, tq, LANES), F32),
            pltpu.VMEM((2, tq, LANES), F32),
            pltpu.VMEM((2, tq, LANES), F32),
        ],
        compiler_params=pltpu.CompilerParams(
            dimension_semantics=("arbitrary", "arbitrary", "arbitrary"),
            vmem_limit_bytes=VMEM_LIMIT),
        name="mla",
    )(qm, km, vm)


def _merge_kernel(x_ref, g_ref, wg_ref, bg_ref, ya_ref, yb_ref, wa_ref, wb_ref, wo_ref, o_ref):
    x = x_ref[...]
    hb = _rms(x, g_ref[...]).astype(BF16)
    gl = jnp.dot(hb, wg_ref[...], preferred_element_type=F32) + bg_ref[...]
    gates = 1.0 / (1.0 + jnp.exp(-gl))
    pa = jnp.dot(ya_ref[...], wa_ref[...], preferred_element_type=F32)
    pb = jnp.dot(yb_ref[...], wb_ref[...], preferred_element_type=F32)
    merged = gates[:, :D_MODEL] * pa + gates[:, D_MODEL:] * pb
    o_ref[...] = x + jnp.dot(merged.astype(BF16), wo_ref[...], preferred_element_type=F32)


def _merge(x2d, g, wg, bg, ya, yb, wa, wb, wo, tm):
    t = x2d.shape[0]
    const = lambda i: (0, 0)
    return pl.pallas_call(
        _merge_kernel,
        grid=(t // tm,),
        in_specs=[
            pl.BlockSpec((tm, D_MODEL), lambda i: (i, 0)),
            pl.BlockSpec((1, D_MODEL), const),
            pl.BlockSpec(wg.shape, const),
            pl.BlockSpec((1, N_BRANCHES * D_MODEL), const),
            pl.BlockSpec((tm, A_WIDTH), lambda i: (i, 0)),
            pl.BlockSpec((tm, B_WIDTH), lambda i: (i, 0)),
            pl.BlockSpec(wa.shape, const),
            pl.BlockSpec(wb.shape, const),
            pl.BlockSpec(wo.shape, const),
        ],
        out_specs=pl.BlockSpec((tm, D_MODEL), lambda i: (i, 0)),
        out_shape=jax.ShapeDtypeStruct((t, D_MODEL), F32),
        compiler_params=pltpu.CompilerParams(
            dimension_semantics=("arbitrary",), vmem_limit_bytes=VMEM_LIMIT),
        name="merge",
    )(x2d, g, wg, bg, ya, yb, wa, wb, wo)


def _mlp_kernel(x_ref, g_ref, w1_ref, w2_ref, gf_ref, o_ref, *, final, fc):
    x = x_ref[...]
    hb = _rms(x, g_ref[...]).astype(BF16)
    acc = jnp.zeros(x.shape, F32)
    for j in range(D_FF // fc):
        a = jnp.dot(hb, w1_ref[:, j * fc:(j + 1) * fc], preferred_element_type=F32)
        a = jnp.square(jnp.maximum(a, 0.0)).astype(BF16)
        acc = acc + jnp.dot(a, w2_ref[j * fc:(j + 1) * fc, :], preferred_element_type=F32)
    y = x + acc
    if final:
        y = _rms(y, gf_ref[...])
    o_ref[...] = y


def _mlp(x2d, g, w1, w2, gf, tm, final):
    t = x2d.shape[0]
    const = lambda i: (0, 0)
    return pl.pallas_call(
        functools.partial(_mlp_kernel, final=final, fc=1024),
        grid=(t // tm,),
        in_specs=[
            pl.BlockSpec((tm, D_MODEL), lambda i: (i, 0)),
            pl.BlockSpec((1, D_MODEL), const),
            pl.BlockSpec(w1.shape, const, pipeline_mode=pl.Buffered(1)),
            pl.BlockSpec(w2.shape, const, pipeline_mode=pl.Buffered(1)),
            pl.BlockSpec((1, D_MODEL), const),
        ],
        out_specs=pl.BlockSpec((tm, D_MODEL), lambda i: (i, 0)),
        out_shape=jax.ShapeDtypeStruct((t, D_MODEL), F32),
        compiler_params=pltpu.CompilerParams(
            dimension_semantics=("arbitrary",), vmem_limit_bytes=VMEM_LIMIT),
        name="mlp",
    )(x2d, g, w1, w2, gf)


def _split_w_in(w_in):
    cuts = [0]
    for s in IN_SIZES:
        cuts.append(cuts[-1] + s)
    return [w_in[..., cuts[n]:cuts[n + 1]] for n in range(len(IN_SIZES))]


def _prep_weights(w_in, w_uq, w_ukv):
    depth = w_in.shape[0]
    a_q, a_k, a_v, i_q, i_k, i_w, c_q, c_kv, k_r, w_g = _split_w_in(w_in)
    z64 = jnp.zeros((depth, D_MODEL, 64), F32)
    w1 = jnp.concatenate([a_q * (LOG2E * A_HEAD_DIM ** -0.5), i_q * IDX_DIM ** -0.5, c_q], axis=-1)
    w2 = jnp.concatenate([
        i_k, z64, z64, i_k, a_k, z64, z64, a_k, a_v, z64, z64, a_v, c_kv,
        z64, k_r, jnp.zeros((depth, D_MODEL, 32), F32),
        i_w * IDX_HEADS ** -0.5, jnp.zeros((depth, D_MODEL, LANES - IDX_HEADS), F32)], axis=-1)
    assert w2.shape[-1] == W2_COLS
    wuq = w_uq.reshape(depth, Q_LORA, B_HEADS, B_QK_DIM)
    wuq = jnp.pad(wuq, ((0, 0), (0, 0), (0, 0), (0, LANES - B_QK_DIM)))
    wuq = wuq.reshape(depth, Q_LORA, B_HEADS * LANES)
    wukv = w_ukv.reshape(depth, KV_LORA, B_HEADS, B_NOPE_DIM + B_V_DIM)
    wk = jnp.pad(wukv[..., :B_NOPE_DIM], ((0, 0), (0, 0), (0, 0), (0, LANES - B_NOPE_DIM)))
    wk = wk.reshape(depth, KV_LORA, B_HEADS * LANES)
    wv = wukv[..., B_NOPE_DIM:].reshape(depth, KV_LORA, B_HEADS * B_V_DIM)
    wukv = jnp.concatenate([wk, wv], axis=-1)
    return (w1.astype(BF16), w2.astype(BF16), w_g.astype(BF16), wuq.astype(BF16),
            wukv.astype(BF16))


def _rope_tables(positions):
    inv_freq = ROPE_THETA ** (-jnp.arange(0, B_ROPE_DIM, 2, dtype=F32) / B_ROPE_DIM)
    ang = positions.astype(F32).reshape(-1)[:, None] * inv_freq
    cos, sin = jnp.cos(ang), jnp.sin(ang)
    t = ang.shape[0]
    half = B_ROPE_DIM // 2
    one = jnp.ones((t, B_NOPE_DIM), F32)
    z = lambda n: jnp.zeros((t, n), F32)
    cs = jnp.concatenate([one, cos, cos, z(LANES - B_QK_DIM)], axis=1)
    s1 = jnp.concatenate([z(B_NOPE_DIM + half), sin, z(LANES - B_QK_DIM)], axis=1)
    s2 = jnp.concatenate([z(B_NOPE_DIM), -sin, z(half + LANES - B_QK_DIM)], axis=1)
    return jnp.stack([cs, s1, s2])


def _rel_bucket(dist):
    max_exact = REL_BUCKETS // 2
    n = jnp.maximum(dist, 0)
    nf = jnp.maximum(n.astype(F32), 1.0)
    log_b = max_exact + (jnp.log(nf / max_exact) / math.log(REL_MAX_DIST / max_exact)
                         * (REL_BUCKETS - max_exact)).astype(I32)
    return jnp.where(n < max_exact, n, jnp.minimum(log_b, REL_BUCKETS - 1))


def _bias_tables(rel_bias):
    rel = rel_bias - rel_bias[REL_BUCKETS - 1][None, :]
    q = jnp.arange(QB, dtype=I32)[:, None]
    s = jnp.arange(KC, dtype=I32)[None, :]

    def lookup(dist):
        bucket = _rel_bucket(dist)[..., None]
        out = jnp.zeros((QB, KC, A_HEADS), F32)
        for b in range(REL_BUCKETS - 1):
            out = jnp.where(bucket == b, rel[b][None, None, :], out)
        return out

    diag = jnp.where((q >= s)[..., None], lookup(q - s), 0.0)
    prev = lookup(KC + q - s)
    table = jnp.stack([prev, diag])
    table = table.reshape(2, QB, KC, NPAIR, 2)
    table = jnp.transpose(table, (0, 4, 3, 1, 2))
    return (table.reshape(2, 2, NPAIR * QB, KC) * LOG2E).astype(F32)


def kernel(x, positions, attn_norm_g, w_in, b_gate, q_latent_norm_g, kv_latent_norm_g, w_uq,
           w_ukv, w_branch_a, w_branch_b, w_out, mlp_norm_g, w_ff1, w_ff2, rel_bias,
           final_norm_g):
    bsz, seq, _ = x.shape
    depth = w_in.shape[0]
    t = bsz * seq
    k_top = min(TOPK_MAX, seq // 4)
    tm = min(512, seq)
    tq = min(512, seq)

    w1, w2, wg, wuq, wukv = _prep_weights(w_in, w_uq, w_ukv)
    wa, wb, wo = w_branch_a.astype(BF16), w_branch_b.astype(BF16), w_out.astype(BF16)
    wf1, wf2 = w_ff1.astype(BF16), w_ff2.astype(BF16)
    rope = _rope_tables(positions)
    bias = _bias_tables(rel_bias)
    gf = final_norm_g.reshape(1, D_MODEL)

    x2d = x.reshape(t, D_MODEL)
    for l in range(depth):
        g_attn = attn_norm_g[l].reshape(1, D_MODEL)
        aq, iq, kvi, iw, qm, km, vm = _proj(
            x2d, g_attn, w1[l], w2[l], q_latent_norm_g[l].reshape(1, Q_LORA),
            kv_latent_norm_g[l].reshape(1, KV_LORA), wuq[l], wukv[l], rope, bsz, seq, tm)
        ya = _dsa(aq, iq, iw, kvi, bias, bsz, seq, k_top)
        yb = _mla(qm, km, vm, tq).reshape(t, B_WIDTH)
        x2d = _merge(x2d, g_attn, wg[l], b_gate[l].reshape(1, N_BRANCHES * D_MODEL), ya, yb,
                     wa[l], wb[l], wo[l], tm)
        x2d = _mlp(x2d, mlp_norm_g[l].reshape(1, D_MODEL), wf1[l], wf2[l], gf, tm,
                   final=(l == depth - 1))
    return x2d.reshape(bsz, seq, D_MODEL)
```

```python
import functools
import math

import jax
import jax.numpy as jnp
from jax import lax
from jax.experimental import pallas as pl
from jax.experimental.pallas import tpu as pltpu

F32 = jnp.float32
BF16 = jnp.bfloat16
I32 = jnp.int32
I16 = jnp.int16

D_MODEL = 1024
A_HEADS = 8
A_HEAD_DIM = 64
A_WIDTH = A_HEADS * A_HEAD_DIM
IDX_HEADS = 4
IDX_DIM = 64
TOPK_MAX = 256
B_HEADS = 8
B_NOPE_DIM = 64
B_ROPE_DIM = 32
B_QK_DIM = B_NOPE_DIM + B_ROPE_DIM
B_V_DIM = 64
B_WIDTH = B_HEADS * B_V_DIM
Q_LORA = 256
KV_LORA = 128
ROPE_THETA = 10000.0
REL_BUCKETS = 32
REL_MAX_DIST = 128
N_BRANCHES = 2
D_FF = 4 * D_MODEL
EPS = 1e-6
NEG_INF = -1e30
LOG2E = math.log2(math.e)
IN_SIZES = (A_WIDTH, A_HEAD_DIM, A_HEAD_DIM, IDX_HEADS * IDX_DIM, IDX_DIM, IDX_HEADS,
            Q_LORA, KV_LORA, B_ROPE_DIM, N_BRANCHES * D_MODEL)

LANES = 128
SUBLANES = 8
QB = 2 * LANES
KC = QB
NPAIR = A_HEADS // 2
INT_MIN = -2 ** 31
I16_MIN = -2 ** 15
PACKED_ROWS = 2 * SUBLANES
G_IK0, G_IK1, G_KK0, G_KK1, G_VA, G_VB = range(6)
W2_COLS = 9 * LANES
VMEM_LIMIT = 56 * 1024 * 1024

NT_DIMS = (((1,), (1,)), ((), ()))


def _rms(x, g):
    return x * lax.rsqrt(jnp.mean(x * x, axis=-1, keepdims=True) + EPS) * g


def _loop_quads(n, body):
    def quad(p, carry):
        for u in range(4):
            body(4 * p + u)
        return carry

    lax.fori_loop(0, n // 4, quad, 0)
    done = (n // 4) * 4

    @pl.when(n % 4 >= 2)
    def _():
        body(done)
        body(done + 1)

    @pl.when(n % 2 == 1)
    def _():
        body(n - 1)


def _proj_kernel(x_ref, g_ref, w1_ref, w2_ref, gq_ref, gkv_ref, wuq_ref, wukv_ref, rope_ref,
                 aq_ref, iq_ref, kvi_ref, iw_ref, qm_ref, km_ref, vm_ref):
    tm = x_ref.shape[0]
    hb = _rms(x_ref[...], g_ref[...]).astype(BF16)
    z1 = jnp.dot(hb, w1_ref[...], preferred_element_type=F32)
    z2 = jnp.dot(hb, w2_ref[...], preferred_element_type=F32)
    aq_ref[...] = z1[:, :A_WIDTH].astype(BF16)
    iq_ref[...] = z1[:, A_WIDTH:A_WIDTH + IDX_HEADS * IDX_DIM].astype(BF16)
    lane = lax.broadcasted_iota(I32, (tm, LANES), 1)
    for gi in range(6):
        blk = z2[:, gi * LANES:(gi + 1) * LANES]
        if gi == G_VA:
            blk = jnp.where(lane == A_HEAD_DIM, 1.0, blk)
        if gi == G_VB:
            blk = jnp.where(lane == 0, 1.0, blk)
        kvi_ref[gi] = blk.astype(BF16)
    iw_ref[...] = z2[:, 8 * LANES:9 * LANES]

    cs, s1, s2 = rope_ref[0], rope_ref[1], rope_ref[2]

    def rope(u):
        return u * cs + pltpu.roll(u, 16, 1) * s1 + pltpu.roll(u, LANES - 16, 1) * s2

    cq = z1[:, A_WIDTH + IDX_HEADS * IDX_DIM:]
    qup = jnp.dot(_rms(cq, gq_ref[...]).astype(BF16), wuq_ref[...], preferred_element_type=F32)
    scale = LOG2E * B_QK_DIM ** -0.5
    for h in range(B_HEADS):
        qm_ref[0, h] = (rope(qup[:, h * LANES:(h + 1) * LANES]) * scale).astype(BF16)
    ckv = z2[:, 6 * LANES:7 * LANES]
    kvup = jnp.dot(_rms(ckv, gkv_ref[...]).astype(BF16), wukv_ref[...],
                   preferred_element_type=F32)
    kr = rope(z2[:, 7 * LANES:8 * LANES])
    for h in range(B_HEADS):
        km_ref[0, h] = (kvup[:, h * LANES:(h + 1) * LANES] + kr).astype(BF16)
    for p in range(B_HEADS // 2):
        c0 = B_HEADS * LANES + p * LANES
        vm_ref[0, p] = kvup[:, c0:c0 + LANES].astype(BF16)


def _proj(x2d, g, w1, w2, gq, gkv, wuq, wukv, rope, bsz, seq, tm):
    t = x2d.shape[0]
    nt = seq // tm
    const = lambda i: (0, 0)
    return pl.pallas_call(
        _proj_kernel,
        grid=(t // tm,),
        in_specs=[
            pl.BlockSpec((tm, D_MODEL), lambda i: (i, 0)),
            pl.BlockSpec((1, D_MODEL), const),
            pl.BlockSpec(w1.shape, const),
            pl.BlockSpec(w2.shape, const),
            pl.BlockSpec((1, Q_LORA), const),
            pl.BlockSpec((1, KV_LORA), const),
            pl.BlockSpec(wuq.shape, const),
            pl.BlockSpec(wukv.shape, const),
            pl.BlockSpec((3, tm, LANES), lambda i: (0, i, 0)),
        ],
        out_specs=[
            pl.BlockSpec((tm, A_WIDTH), lambda i: (i, 0)),
            pl.BlockSpec((tm, IDX_HEADS * IDX_DIM), lambda i: (i, 0)),
            pl.BlockSpec((6, tm, LANES), lambda i: (0, i, 0)),
            pl.BlockSpec((tm, LANES), lambda i: (i, 0)),
            pl.BlockSpec((1, B_HEADS, tm, LANES), lambda i: (i // nt, 0, i % nt, 0)),
            pl.BlockSpec((1, B_HEADS, tm, LANES), lambda i: (i // nt, 0, i % nt, 0)),
            pl.BlockSpec((1, B_HEADS // 2, tm, LANES), lambda i: (i // nt, 0, i % nt, 0)),
        ],
        out_shape=[
            jax.ShapeDtypeStruct((t, A_WIDTH), BF16),
            jax.ShapeDtypeStruct((t, IDX_HEADS * IDX_DIM), BF16),
            jax.ShapeDtypeStruct((6, t, LANES), BF16),
            jax.ShapeDtypeStruct((t, LANES), F32),
            jax.ShapeDtypeStruct((bsz, B_HEADS, seq, LANES), BF16),
            jax.ShapeDtypeStruct((bsz, B_HEADS, seq, LANES), BF16),
            jax.ShapeDtypeStruct((bsz, B_HEADS // 2, seq, LANES), BF16),
        ],
        compiler_params=pltpu.CompilerParams(
            dimension_semantics=("arbitrary",), vmem_limit_bytes=VMEM_LIMIT),
        name="proj",
    )(x2d, g, w1, w2, gq, gkv, wuq, wukv, rope)


def _dsa_kernel(aq_ref, iq_ref, iw_ref, kvi_ref, bias_ref, out_ref,
                key_ref, hi_ref, lo_ref, tie_ref, lg_ref, mx_ref, acc_ref, *, k_top):
    i = pl.program_id(1)
    n2 = i + 1
    kf = float(k_top)
    nrow = NPAIR * QB

    row = lax.broadcasted_iota(I32, (KC, QB), 0)
    col = lax.broadcasted_iota(I32, (KC, QB), 1)

    def chunk(c):
        return pl.ds(pl.multiple_of(c * KC, KC), KC)

    iq = iq_ref[...]
    iqs = jnp.concatenate([iq[:, :LANES], iq[:, LANES:]], axis=0)
    iw_t = iw_ref[...].T
    w0, w1, w2, w3 = (iw_t[h:h + 1, :] for h in range(IDX_HEADS))

    def score_body(c, carry):
        t0 = lax.dot_general(kvi_ref[G_IK0, chunk(c), :], iqs, NT_DIMS,
                             preferred_element_type=F32)
        t1 = lax.dot_general(kvi_ref[G_IK1, chunk(c), :], iqs, NT_DIMS,
                             preferred_element_type=F32)
        s = (w0 * jnp.maximum(t0[:, :QB], 0.0) + w1 * jnp.maximum(t1[:, :QB], 0.0)
             + w2 * jnp.maximum(t0[:, QB:], 0.0) + w3 * jnp.maximum(t1[:, QB:], 0.0))
        b = lax.bitcast_convert_type(s, I32)
        sk = jnp.where(b < 0, b ^ 0x7FFFFFFF, b)
        sk = jnp.where(c * KC + row > i * QB + col, INT_MIN, sk)
        key_ref[chunk(c), :] = sk
        hi_ref[chunk(c), :] = (sk >> 16).astype(I16)
        lo_ref[chunk(c), :] = ((sk & 0xFFFF) + I16_MIN).astype(I16)
        return carry

    _loop_quads(n2, lambda c: score_body(c, 0))

    def count16(ref, cand):
        cb = jnp.broadcast_to(cand.astype(I16), (PACKED_ROWS, QB))

        def body(c, accs):
            keys = ref[chunk(c), :]
            accs = list(accs)
            for r in range(KC // PACKED_ROWS):
                blk = keys[r * PACKED_ROWS:(r + 1) * PACKED_ROWS, :]
                accs[r % 4] = accs[r % 4] + jnp.where(blk >= cb, jnp.int16(1), jnp.int16(0))
            return tuple(accs)

        z = jnp.zeros((PACKED_ROWS, QB), I16)
        a = lax.fori_loop(0, n2, body, (z, z, z, z))
        tot = (a[0] + a[1]) + (a[2] + a[3])
        return jnp.sum(tot.astype(F32), axis=0, keepdims=True)

    def bisect16(ref):
        def bit_body(it, base):
            cand = base + lax.shift_left(jnp.int32(1), 15 - it)
            return jnp.where(count16(ref, cand) >= kf, cand, base)

        return lax.fori_loop(0, 16, bit_body, jnp.full((1, QB), I16_MIN, I32))

    tau_hi = bisect16(hi_ref)
    th = tau_hi.astype(I16)

    def refine_body(c, carry):
        hi = hi_ref[chunk(c), :]
        lo_ref[chunk(c), :] = jnp.where(
            hi == th, lo_ref[chunk(c), :],
            jnp.where(hi > th, jnp.int16(-I16_MIN - 1), jnp.int16(I16_MIN)))
        return carry

    lax.fori_loop(0, n2, refine_body, 0)
    tau = tau_hi * 65536 + (bisect16(lo_ref) - I16_MIN)

    def count_gt(cand):
        cb = jnp.broadcast_to(cand, (SUBLANES, QB))

        def body(c, accs):
            keys = key_ref[chunk(c), :]
            accs = list(accs)
            for r in range(KC // SUBLANES):
                blk = keys[r * SUBLANES:(r + 1) * SUBLANES, :]
                accs[r % 4] = accs[r % 4] + jnp.where(blk > cb, 1.0, 0.0)
            return tuple(accs)

        z = jnp.zeros((SUBLANES, QB), F32)
        a = lax.fori_loop(0, n2, body, (z, z, z, z))
        return jnp.sum((a[0] + a[1]) + (a[2] + a[3]), axis=0, keepdims=True)

    need = kf - count_gt(tau)

    ltri = jnp.where(row >= col, 1.0, 0.0).astype(BF16)

    def mask_chunk(c):
        blk = key_ref[chunk(c), :]
        eq = blk == tau
        pref = jnp.dot(ltri, jnp.where(eq, 1.0, 0.0).astype(BF16),
                       preferred_element_type=F32)
        carry = tie_ref[...]
        sel = (blk > tau) | (eq & (pref + carry <= need))
        sel = sel & (c * KC + row <= i * QB + col)
        tie_ref[...] = carry + pref[KC - 1:KC, :]
        return jnp.where(sel, jnp.inf, NEG_INF).T

    tie_ref[...] = jnp.zeros(tie_ref.shape, F32)

    aq = aq_ref[...]
    qs = jnp.concatenate([aq[:, j * LANES:(j + 1) * LANES] for j in range(NPAIR)], axis=0)
    mx_ref[...] = jnp.full(mx_ref.shape, -jnp.inf, F32)

    def logits_chunk(c2, bias_pos):
        mc = mask_chunk(c2)
        for r, g in ((0, G_KK0), (1, G_KK1)):
            lg = lax.dot_general(qs, kvi_ref[g, chunk(c2), :], NT_DIMS,
                                 preferred_element_type=F32)
            if bias_pos is not None:
                lg = lg + bias_ref[bias_pos, r]
            for j in range(NPAIR):
                rows = slice(j * QB, (j + 1) * QB)
                lj = jnp.minimum(lg[rows], mc)
                lg_ref[r, c2, rows, :] = lj
                mrows = slice(r * nrow + j * QB, r * nrow + (j + 1) * QB)
                mx_ref[mrows, :] = jnp.maximum(mx_ref[mrows, :],
                                               jnp.maximum(lj[:, :LANES], lj[:, LANES:]))

    _loop_quads(jnp.maximum(n2 - 2, 0), lambda c2: logits_chunk(c2, None))

    @pl.when(n2 >= 2)
    def _():
        logits_chunk(n2 - 2, 0)
        logits_chunk(n2 - 1, 1)

    @pl.when(n2 == 1)
    def _():
        logits_chunk(0, 1)

    mrow = jnp.max(mx_ref[...], axis=1, keepdims=True)
    acc_ref[...] = jnp.zeros(acc_ref.shape, F32)

    def pv_body(c2, carry):
        for r, g in ((0, G_VA), (1, G_VB)):
            p = jnp.exp2(lg_ref[r, c2] - mrow[r * nrow:(r + 1) * nrow])
            acc_ref[r * nrow:(r + 1) * nrow, :] += jnp.dot(
                p.astype(BF16), kvi_ref[g, chunk(c2), :], preferred_element_type=F32)
        return carry

    _loop_quads(n2, lambda c2: pv_body(c2, 0))

    lane = lax.broadcasted_iota(I32, (QB, LANES), 1)
    for j in range(NPAIR):
        oe = acc_ref[j * QB:(j + 1) * QB, :]
        oo = acc_ref[nrow + j * QB:nrow + (j + 1) * QB, :]
        out = jnp.where(lane < A_HEAD_DIM, oe / oe[:, A_HEAD_DIM:A_HEAD_DIM + 1], oo / oo[:, 0:1])
        out_ref[:, j * LANES:(j + 1) * LANES] = out.astype(BF16)


def _dsa(aq, iq, iw, kvi, bias, bsz, seq, k_top):
    t = aq.shape[0]
    nq = seq // QB
    return pl.pallas_call(
        functools.partial(_dsa_kernel, k_top=k_top),
        grid=(bsz, nq),
        in_specs=[
            pl.BlockSpec((QB, A_WIDTH), lambda b, i: (b * nq + i, 0)),
            pl.BlockSpec((QB, IDX_HEADS * IDX_DIM), lambda b, i: (b * nq + i, 0)),
            pl.BlockSpec((QB, LANES), lambda b, i: (b * nq + i, 0)),
            pl.BlockSpec((6, seq, LANES), lambda b, i: (0, b, 0)),
            pl.BlockSpec(bias.shape, lambda b, i: (0, 0, 0, 0)),
        ],
        out_specs=pl.BlockSpec((QB, A_WIDTH), lambda b, i: (b * nq + i, 0)),
        out_shape=jax.ShapeDtypeStruct((t, A_WIDTH), BF16),
        scratch_shapes=[
            pltpu.VMEM((seq, QB), I32),
            pltpu.VMEM((seq, QB), I16),
            pltpu.VMEM((seq, QB), I16),
            pltpu.VMEM((1, QB), F32),
            pltpu.VMEM((2, nq, NPAIR * QB, KC), F32),
            pltpu.VMEM((A_HEADS * QB, LANES), F32),
            pltpu.VMEM((A_HEADS * QB, LANES), F32),
        ],
        compiler_params=pltpu.CompilerParams(
            dimension_semantics=("arbitrary", "arbitrary"), vmem_limit_bytes=VMEM_LIMIT),
        name="dsa",
    )(aq, iq, iw, kvi, bias)


def _mla_kernel(q_ref, k_ref, v_ref, out_ref, lg_ref, mx_ref, ls_ref, acc_ref):
    tq = q_ref.shape[2]
    lane = lax.broadcasted_iota(I32, (tq, LANES), 1)

    def lane_fold(x, op):
        part = x[:, :LANES]
        for s in range(1, tq // LANES):
            part = op(part, x[:, s * LANES:(s + 1) * LANES])
        return part

    def logits(c, diag):
        for e in range(2):
            lg = lax.dot_general(q_ref[0, e], k_ref[0, e, c * tq:(c + 1) * tq, :], NT_DIMS,
                                 preferred_element_type=F32)
            if diag:
                row = lax.broadcasted_iota(I32, (tq, tq), 0)
                col = lax.broadcasted_iota(I32, (tq, tq), 1)
                lg = jnp.where(col <= row, lg, NEG_INF)
            lg_ref[e, c] = lg
            part = lane_fold(lg, jnp.maximum)
            mx_ref[e] = part if c == 0 else jnp.maximum(mx_ref[e], part)

    def pv(c, m):
        v = v_ref[0, 0, c * tq:(c + 1) * tq, :]
        for e in range(2):
            p = jnp.exp2(lg_ref[e, c] - m[e])
            part = lane_fold(p, jnp.add)
            ls_ref[e] = part if c == 0 else ls_ref[e] + part
            o = jnp.dot(p.astype(BF16), v, preferred_element_type=F32)
            acc_ref[e] = o if c == 0 else acc_ref[e] + o

    def attend(n):
        for c in range(n):
            logits(c, False)
        logits(n, True)
        m = [jnp.max(mx_ref[e], axis=1, keepdims=True) for e in range(2)]
        for c in range(n + 1):
            pv(c, m)
        outs = [acc_ref[e] / jnp.sum(ls_ref[e], axis=1, keepdims=True) for e in range(2)]
        out_ref[0] = jnp.where(lane < B_V_DIM, outs[0], outs[1]).astype(BF16)

    for n in range(k_ref.shape[2] // tq):
        pl.when(pl.program_id(2) == n)(functools.partial(attend, n))


def _mla(qm, km, vm, tq):
    bsz, _, seq, _ = qm.shape
    return pl.pallas_call(
        _mla_kernel,
        grid=(bsz, B_HEADS // 2, seq // tq),
        in_specs=[
            pl.BlockSpec((1, 2, tq, LANES), lambda b, p, i: (b, p, i, 0)),
            pl.BlockSpec((1, 2, seq, LANES), lambda b, p, i: (b, p, 0, 0)),
            pl.BlockSpec((1, 1, seq, LANES), lambda b, p, i: (b, p, 0, 0)),
        ],
        out_specs=pl.BlockSpec((1, tq, LANES), lambda b, p, i: (b, i, p)),
        out_shape=jax.ShapeDtypeStruct((bsz, seq, B_WIDTH), BF16),
        scratch_shapes=[
            pltpu.VMEM((2, seq // tq, tq, tq), F32),
            pltpu.VMEM((2, tq, LANES), F32),
            pltpu.VMEM((2, tq, LANES), F32),
            pltpu.VMEM((2, tq, LANES), F32),
        ],
        compiler_params=pltpu.CompilerParams(
            dimension_semantics=("arbitrary", "arbitrary", "arbitrary"),
            vmem_limit_bytes=VMEM_LIMIT),
        name="mla",
    )(qm, km, vm)


def _merge_kernel(x_ref, g_ref, wg_ref, bg_ref, ya_ref, yb_ref, wa_ref, wb_ref, wo_ref, o_ref):
    x = x_ref[...]
    hb = _rms(x, g_ref[...]).astype(BF16)
    gl = jnp.dot(hb, wg_ref[...], preferred_element_type=F32) + bg_ref[...]
    gates = 1.0 / (1.0 + jnp.exp(-gl))
    pa = jnp.dot(ya_ref[...], wa_ref[...], preferred_element_type=F32)
    pb = jnp.dot(yb_ref[...], wb_ref[...], preferred_element_type=F32)
    merged = gates[:, :D_MODEL] * pa + gates[:, D_MODEL:] * pb
    o_ref[...] = x + jnp.dot(merged.astype(BF16), wo_ref[...], preferred_element_type=F32)


def _merge(x2d, g, wg, bg, ya, yb, wa, wb, wo, tm):
    t = x2d.shape[0]
    const = lambda i: (0, 0)
    return pl.pallas_call(
        _merge_kernel,
        grid=(t // tm,),
        in_specs=[
            pl.BlockSpec((tm, D_MODEL), lambda i: (i, 0)),
            pl.BlockSpec((1, D_MODEL), const),
            pl.BlockSpec(wg.shape, const),
            pl.BlockSpec((1, N_BRANCHES * D_MODEL), const),
            pl.BlockSpec((tm, A_WIDTH), lambda i: (i, 0)),
            pl.BlockSpec((tm, B_WIDTH), lambda i: (i, 0)),
            pl.BlockSpec(wa.shape, const),
            pl.BlockSpec(wb.shape, const),
            pl.BlockSpec(wo.shape, const),
        ],
        out_specs=pl.BlockSpec((tm, D_MODEL), lambda i: (i, 0)),
        out_shape=jax.ShapeDtypeStruct((t, D_MODEL), F32),
        compiler_params=pltpu.CompilerParams(
            dimension_semantics=("arbitrary",), vmem_limit_bytes=VMEM_LIMIT),
        name="merge",
    )(x2d, g, wg, bg, ya, yb, wa, wb, wo)


def _mlp_kernel(x_ref, g_ref, w1_ref, w2_ref, gf_ref, o_ref, *, final, fc):
    x = x_ref[...]
    hb = _rms(x, g_ref[...]).astype(BF16)
    acc = jnp.zeros(x.shape, F32)
    for j in range(D_FF // fc):
        a = jnp.dot(hb, w1_ref[:, j * fc:(j + 1) * fc], preferred_element_type=F32)
        a = jnp.square(jnp.maximum(a, 0.0)).astype(BF16)
        acc = acc + jnp.dot(a, w2_ref[j * fc:(j + 1) * fc, :], preferred_element_type=F32)
    y = x + acc
    if final:
        y = _rms(y, gf_ref[...])
    o_ref[...] = y


def _mlp(x2d, g, w1, w2, gf, tm, final):
    t = x2d.shape[0]
    const = lambda i: (0, 0)
    return pl.pallas_call(
        functools.partial(_mlp_kernel, final=final, fc=1024),
        grid=(t // tm,),
        in_specs=[
            pl.BlockSpec((tm, D_MODEL), lambda i: (i, 0)),
            pl.BlockSpec((1, D_MODEL), const),
            pl.BlockSpec(w1.shape, const, pipeline_mode=pl.Buffered(1)),
            pl.BlockSpec(w2.shape, const, pipeline_mode=pl.Buffered(1)),
            pl.BlockSpec((1, D_MODEL), const),
        ],
        out_specs=pl.BlockSpec((tm, D_MODEL), lambda i: (i, 0)),
        out_shape=jax.ShapeDtypeStruct((t, D_MODEL), F32),
        compiler_params=pltpu.CompilerParams(
            dimension_semantics=("arbitrary",), vmem_limit_bytes=VMEM_LIMIT),
        name="mlp",
    )(x2d, g, w1, w2, gf)


def _split_w_in(w_in):
    cuts = [0]
    for s in IN_SIZES:
        cuts.append(cuts[-1] + s)
    return [w_in[..., cuts[n]:cuts[n + 1]] for n in range(len(IN_SIZES))]


def _prep_weights(w_in, w_uq, w_ukv):
    depth = w_in.shape[0]
    a_q, a_k, a_v, i_q, i_k, i_w, c_q, c_kv, k_r, w_g = _split_w_in(w_in)
    z64 = jnp.zeros((depth, D_MODEL, 64), F32)
    w1 = jnp.concatenate([a_q * (LOG2E * A_HEAD_DIM ** -0.5), i_q * IDX_DIM ** -0.5, c_q], axis=-1)
    w2 = jnp.concatenate([
        i_k, z64, z64, i_k, a_k, z64, z64, a_k, a_v, z64, z64, a_v, c_kv,
        z64, k_r, jnp.zeros((depth, D_MODEL, 32), F32),
        i_w * IDX_HEADS ** -0.5, jnp.zeros((depth, D_MODEL, LANES - IDX_HEADS), F32)], axis=-1)
    assert w2.shape[-1] == W2_COLS
    wuq = w_uq.reshape(depth, Q_LORA, B_HEADS, B_QK_DIM)
    wuq = jnp.pad(wuq, ((0, 0), (0, 0), (0, 0), (0, LANES - B_QK_DIM)))
    wuq = wuq.reshape(depth, Q_LORA, B_HEADS * LANES)
    wukv = w_ukv.reshape(depth, KV_LORA, B_HEADS, B_NOPE_DIM + B_V_DIM)
    wk = jnp.pad(wukv[..., :B_NOPE_DIM], ((0, 0), (0, 0), (0, 0), (0, LANES - B_NOPE_DIM)))
    wk = wk.reshape(depth, KV_LORA, B_HEADS * LANES)
    wv = wukv[..., B_NOPE_DIM:].reshape(depth, KV_LORA, B_HEADS * B_V_DIM)
    wukv = jnp.concatenate([wk, wv], axis=-1)
    return (w1.astype(BF16), w2.astype(BF16), w_g.astype(BF16), wuq.astype(BF16),
            wukv.astype(BF16))


def _rope_tables(positions):
    inv_freq = ROPE_THETA ** (-jnp.arange(0, B_ROPE_DIM, 2, dtype=F32) / B_ROPE_DIM)
    ang = positions.astype(F32).reshape(-1)[:, None] * inv_freq
    cos, sin = jnp.cos(ang), jnp.sin(ang)
    t = ang.shape[0]
    half = B_ROPE_DIM // 2
    one = jnp.ones((t, B_NOPE_DIM), F32)
    z = lambda n: jnp.zeros((t, n), F32)
    cs = jnp.concatenate([one, cos, cos, z(LANES - B_QK_DIM)], axis=1)
    s1 = jnp.concatenate([z(B_NOPE_DIM + half), sin, z(LANES - B_QK_DIM)], axis=1)
    s2 = jnp.concatenate([z(B_NOPE_DIM), -sin, z(half + LANES - B_QK_DIM)], axis=1)
    return jnp.stack([cs, s1, s2])


def _rel_bucket(dist):
    max_exact = REL_BUCKETS // 2
    n = jnp.maximum(dist, 0)
    nf = jnp.maximum(n.astype(F32), 1.0)
    log_b = max_exact + (jnp.log(nf / max_exact) / math.log(REL_MAX_DIST / max_exact)
                         * (REL_BUCKETS - max_exact)).astype(I32)
    return jnp.where(n < max_exact, n, jnp.minimum(log_b, REL_BUCKETS - 1))


def _bias_tables(rel_bias):
    rel = rel_bias - rel_bias[REL_BUCKETS - 1][None, :]
    q = jnp.arange(QB, dtype=I32)[:, None]
    s = jnp.arange(KC, dtype=I32)[None, :]

    def lookup(dist):
        bucket = _rel_bucket(dist)[..., None]
        out = jnp.zeros((QB, KC, A_HEADS), F32)
        for b in range(REL_BUCKETS - 1):
            out = jnp.where(bucket == b, rel[b][None, None, :], out)
        return out

    diag = jnp.where((q >= s)[..., None], lookup(q - s), 0.0)
    prev = lookup(KC + q - s)
    table = jnp.stack([prev, diag])
    table = table.reshape(2, QB, KC, NPAIR, 2)
    table = jnp.transpose(table, (0, 4, 3, 1, 2))
    return (table.reshape(2, 2, NPAIR * QB, KC) * LOG2E).astype(F32)


def kernel(x, positions, attn_norm_g, w_in, b_gate, q_latent_norm_g, kv_latent_norm_g, w_uq,
           w_ukv, w_branch_a, w_branch_b, w_out, mlp_norm_g, w_ff1, w_ff2, rel_bias,
           final_norm_g):
    bsz, seq, _ = x.shape
    depth = w_in.shape[0]
    t = bsz * seq
    k_top = min(TOPK_MAX, seq // 4)
    tm = min(512, seq)
    tq = min(512, seq)

    w1, w2, wg, wuq, wukv = _prep_weights(w_in, w_uq, w_ukv)
    wa, wb, wo = w_branch_a.astype(BF16), w_branch_b.astype(BF16), w_out.astype(BF16)
    wf1, wf2 = w_ff1.astype(BF16), w_ff2.astype(BF16)
    rope = _rope_tables(positions)
    bias = _bias_tables(rel_bias)
    gf = final_norm_g.reshape(1, D_MODEL)

    x2d = x.reshape(t, D_MODEL)
    for l in range(depth):
        g_attn = attn_norm_g[l].reshape(1, D_MODEL)
        aq, iq, kvi, iw, qm, km, vm = _proj(
            x2d, g_attn, w1[l], w2[l], q_latent_norm_g[l].reshape(1, Q_LORA),
            kv_latent_norm_g[l].reshape(1, KV_LORA), wuq[l], wukv[l], rope, bsz, seq, tm)
        ya = _dsa(aq, iq, iw, kvi, bias, bsz, seq, k_top)
        yb = _mla(qm, km, vm, tq).reshape(t, B_WIDTH)
        x2d = _merge(x2d, g_attn, wg[l], b_gate[l].reshape(1, N_BRANCHES * D_MODEL), ya, yb,
                     wa[l], wb[l], wo[l], tm)
        x2d = _mlp(x2d, mlp_norm_g[l].reshape(1, D_MODEL), wf1[l], wf2[l], gf, tm,
                   final=(l == depth - 1))
    return x2d.reshape(bsz, seq, D_MODEL)
```

```python
import functools
import math

import jax
import jax.numpy as jnp
from jax import lax
from jax.experimental import pallas as pl
from jax.experimental.pallas import tpu as pltpu

F32 = jnp.float32
BF16 = jnp.bfloat16
I32 = jnp.int32
I16 = jnp.int16

D_MODEL = 1024
A_HEADS = 8
A_HEAD_DIM = 64
A_WIDTH = A_HEADS * A_HEAD_DIM
IDX_HEADS = 4
IDX_DIM = 64
TOPK_MAX = 256
B_HEADS = 8
B_NOPE_DIM = 64
B_ROPE_DIM = 32
B_QK_DIM = B_NOPE_DIM + B_ROPE_DIM
B_V_DIM = 64
B_WIDTH = B_HEADS * B_V_DIM
Q_LORA = 256
KV_LORA = 128
ROPE_THETA = 10000.0
REL_BUCKETS = 32
REL_MAX_DIST = 128
N_BRANCHES = 2
D_FF = 4 * D_MODEL
EPS = 1e-6
NEG_INF = -1e30
LOG2E = math.log2(math.e)
IN_SIZES = (A_WIDTH, A_HEAD_DIM, A_HEAD_DIM, IDX_HEADS * IDX_DIM, IDX_DIM, IDX_HEADS,
            Q_LORA, KV_LORA, B_ROPE_DIM, N_BRANCHES * D_MODEL)

LANES = 128
SUBLANES = 8
QB = 2 * LANES
KC = QB
NPAIR = A_HEADS // 2
INT_MIN = -2 ** 31
I16_MIN = -2 ** 15
PACKED_ROWS = 2 * SUBLANES
G_IK0, G_IK1, G_KK0, G_KK1, G_VA, G_VB = range(6)
PROJ_SPLIT = 4
W2_COLS = 3 * LANES
IW_LANE = A_HEAD_DIM + B_ROPE_DIM
VMEM_LIMIT = 56 * 1024 * 1024

NT_DIMS = (((1,), (1,)), ((), ()))


def _rms(x, g):
    return x * lax.rsqrt(jnp.mean(x * x, axis=-1, keepdims=True) + EPS) * g


def _loop_quads(n, body):
    def quad(p, carry):
        for u in range(4):
            body(4 * p + u)
        return carry

    lax.fori_loop(0, n // 4, quad, 0)
    done = (n // 4) * 4

    @pl.when(n % 4 >= 2)
    def _():
        body(done)
        body(done + 1)

    @pl.when(n % 2 == 1)
    def _():
        body(n - 1)


def _proj_kernel(x_ref, g_ref, w1_ref, w2_ref, gq_ref, gkv_ref, wuq_ref, wukv_ref, rope_ref,
                 aq_ref, iq_ref, kvi_ref, iw_ref, qm_ref, km_ref, vm_ref):
    tm = x_ref.shape[0] // PROJ_SPLIT
    lane = lax.broadcasted_iota(I32, (tm, LANES), 1)
    low = lane < LANES // 2
    half = B_ROPE_DIM // 2
    for part in range(PROJ_SPLIT):
        rs = slice(part * tm, (part + 1) * tm)
        hb = _rms(x_ref[rs, :], g_ref[...]).astype(BF16)
        z1 = jnp.dot(hb, w1_ref[...], preferred_element_type=F32)
        z2 = jnp.dot(hb, w2_ref[...], preferred_element_type=F32)
        aq_ref[rs, :] = z1[:, :A_WIDTH].astype(BF16)
        iq_ref[rs, :] = z1[:, A_WIDTH:A_WIDTH + IDX_HEADS * IDX_DIM].astype(BF16)
        ga = z2[:, :LANES]
        gb = z2[:, LANES:2 * LANES]
        ra = pltpu.roll(ga, LANES // 2, 1)
        rb = pltpu.roll(gb, LANES // 2, 1)
        kvi_ref[G_IK0, rs, :] = jnp.where(low, ga, 0.0).astype(BF16)
        kvi_ref[G_IK1, rs, :] = jnp.where(low, 0.0, ra).astype(BF16)
        kvi_ref[G_KK0, rs, :] = jnp.where(low, ra, 0.0).astype(BF16)
        kvi_ref[G_KK1, rs, :] = jnp.where(low, 0.0, ga).astype(BF16)
        kvi_ref[G_VA, rs, :] = jnp.where(
            low, gb, jnp.where(lane == A_HEAD_DIM, 1.0, 0.0)).astype(BF16)
        kvi_ref[G_VB, rs, :] = jnp.where(low, jnp.where(lane == 0, 1.0, 0.0), rb).astype(BF16)
        iw_ref[rs, :] = gb

        cs, sn = rope_ref[0, rs, :], rope_ref[1, rs, :]
        cq = z1[:, A_WIDTH + IDX_HEADS * IDX_DIM:]
        qup = jnp.dot(_rms(cq, gq_ref[...]).astype(BF16), wuq_ref[...],
                      preferred_element_type=F32)
        scale = LOG2E * B_QK_DIM ** -0.5
        for h in range(B_HEADS):
            u = qup[:, h * LANES:(h + 1) * LANES]
            rot = u * cs + pltpu.roll(u, LANES - B_ROPE_DIM, 1) * sn
            qm_ref[0, h, rs, :] = (rot * scale).astype(BF16)
        ckv = z2[:, 2 * LANES:3 * LANES]
        kvup = jnp.dot(_rms(ckv, gkv_ref[...]).astype(BF16), wukv_ref[...],
                       preferred_element_type=F32)
        ku = jnp.where((lane >= B_NOPE_DIM) & (lane < B_QK_DIM), gb, 0.0)
        kr = (ku * cs
              + pltpu.roll(ku, half, 1) * jnp.where(lane >= B_NOPE_DIM + half, sn, 0.0)
              + pltpu.roll(ku, LANES - half, 1) * jnp.where(lane < B_NOPE_DIM + half, sn, 0.0))
        for h in range(B_HEADS):
            km_ref[0, h, rs, :] = (kvup[:, h * LANES:(h + 1) * LANES] + kr).astype(BF16)
        for p in range(B_HEADS // 2):
            c0 = B_HEADS * LANES + p * LANES
            vm_ref[0, p, rs, :] = kvup[:, c0:c0 + LANES].astype(BF16)


def _proj(x2d, g, w1, w2, gq, gkv, wuq, wukv, rope, bsz, seq, tm):
    t = x2d.shape[0]
    nt = seq // tm
    const = lambda i: (0, 0)
    return pl.pallas_call(
        _proj_kernel,
        grid=(t // tm,),
        in_specs=[
            pl.BlockSpec((tm, D_MODEL), lambda i: (i, 0)),
            pl.BlockSpec((1, D_MODEL), const),
            pl.BlockSpec(w1.shape, const),
            pl.BlockSpec(w2.shape, const),
            pl.BlockSpec((1, Q_LORA), const),
            pl.BlockSpec((1, KV_LORA), const),
            pl.BlockSpec(wuq.shape, const),
            pl.BlockSpec(wukv.shape, const),
            pl.BlockSpec((2, tm, LANES), lambda i: (0, i, 0)),
        ],
        out_specs=[
            pl.BlockSpec((tm, A_WIDTH), lambda i: (i, 0)),
            pl.BlockSpec((tm, IDX_HEADS * IDX_DIM), lambda i: (i, 0)),
            pl.BlockSpec((6, tm, LANES), lambda i: (0, i, 0)),
            pl.BlockSpec((tm, LANES), lambda i: (i, 0)),
            pl.BlockSpec((1, B_HEADS, tm, LANES), lambda i: (i // nt, 0, i % nt, 0)),
            pl.BlockSpec((1, B_HEADS, tm, LANES), lambda i: (i // nt, 0, i % nt, 0)),
            pl.BlockSpec((1, B_HEADS // 2, tm, LANES), lambda i: (i // nt, 0, i % nt, 0)),
        ],
        out_shape=[
            jax.ShapeDtypeStruct((t, A_WIDTH), BF16),
            jax.ShapeDtypeStruct((t, IDX_HEADS * IDX_DIM), BF16),
            jax.ShapeDtypeStruct((6, t, LANES), BF16),
            jax.ShapeDtypeStruct((t, LANES), F32),
            jax.ShapeDtypeStruct((bsz, B_HEADS, seq, LANES), BF16),
            jax.ShapeDtypeStruct((bsz, B_HEADS, seq, LANES), BF16),
            jax.ShapeDtypeStruct((bsz, B_HEADS // 2, seq, LANES), BF16),
        ],
        compiler_params=pltpu.CompilerParams(
            dimension_semantics=("arbitrary",), vmem_limit_bytes=VMEM_LIMIT),
        name="proj",
    )(x2d, g, w1, w2, gq, gkv, wuq, wukv, rope)


def _dsa_kernel(aq_ref, iq_ref, iw_ref, kvi_ref, bias_ref, out_ref,
                key_ref, hi_ref, lo_ref, tie_ref, lg_ref, mx_ref, acc_ref, *, k_top):
    i = pl.program_id(1)
    n2 = i + 1
    kf = float(k_top)
    nrow = NPAIR * QB

    row = lax.broadcasted_iota(I32, (KC, QB), 0)
    col = lax.broadcasted_iota(I32, (KC, QB), 1)

    def chunk(c):
        return pl.ds(pl.multiple_of(c * KC, KC), KC)

    iq = iq_ref[...]
    iqs = jnp.concatenate([iq[:, :LANES], iq[:, LANES:]], axis=0)
    iw_t = iw_ref[...].T
    w0, w1, w2, w3 = (iw_t[IW_LANE + h:IW_LANE + h + 1, :] for h in range(IDX_HEADS))

    def score_body(c, carry):
        t0 = lax.dot_general(kvi_ref[G_IK0, chunk(c), :], iqs, NT_DIMS,
                             preferred_element_type=F32)
        t1 = lax.dot_general(kvi_ref[G_IK1, chunk(c), :], iqs, NT_DIMS,
                             preferred_element_type=F32)
        s = (w0 * jnp.maximum(t0[:, :QB], 0.0) + w1 * jnp.maximum(t1[:, :QB], 0.0)
             + w2 * jnp.maximum(t0[:, QB:], 0.0) + w3 * jnp.maximum(t1[:, QB:], 0.0))
        b = lax.bitcast_convert_type(s, I32)
        sk = jnp.where(b < 0, b ^ 0x7FFFFFFF, b)
        sk = jnp.where(c * KC + row > i * QB + col, INT_MIN, sk)
        key_ref[chunk(c), :] = sk
        hi_ref[chunk(c), :] = (sk >> 16).astype(I16)
        lo_ref[chunk(c), :] = ((sk & 0xFFFF) + I16_MIN).astype(I16)
        return carry

    _loop_quads(n2, lambda c: score_body(c, 0))

    def count16(ref, cand):
        cb = jnp.broadcast_to(cand.astype(I16), (PACKED_ROWS, QB))

        def body(c, accs):
            keys = ref[chunk(c), :]
            accs = list(accs)
            for r in range(KC // PACKED_ROWS):
                blk = keys[r * PACKED_ROWS:(r + 1) * PACKED_ROWS, :]
                accs[r % 4] = accs[r % 4] + jnp.where(blk >= cb, jnp.int16(1), jnp.int16(0))
            return tuple(accs)

        z = jnp.zeros((PACKED_ROWS, QB), I16)
        a = lax.fori_loop(0, n2, body, (z, z, z, z))
        tot = (a[0] + a[1]) + (a[2] + a[3])
        return jnp.sum(tot.astype(F32), axis=0, keepdims=True)

    def bisect16(ref):
        def bit_body(it, base):
            cand = base + lax.shift_left(jnp.int32(1), 15 - it)
            return jnp.where(count16(ref, cand) >= kf, cand, base)

        return lax.fori_loop(0, 16, bit_body, jnp.full((1, QB), I16_MIN, I32))

    tau_hi = bisect16(hi_ref)
    th = tau_hi.astype(I16)

    def refine_body(c, carry):
        hi = hi_ref[chunk(c), :]
        lo_ref[chunk(c), :] = jnp.where(
            hi == th, lo_ref[chunk(c), :],
            jnp.where(hi > th, jnp.int16(-I16_MIN - 1), jnp.int16(I16_MIN)))
        return carry

    lax.fori_loop(0, n2, refine_body, 0)
    tau = tau_hi * 65536 + (bisect16(lo_ref) - I16_MIN)

    def count_gt(cand):
        cb = jnp.broadcast_to(cand, (SUBLANES, QB))

        def body(c, accs):
            keys = key_ref[chunk(c), :]
            accs = list(accs)
            for r in range(KC // SUBLANES):
                blk = keys[r * SUBLANES:(r + 1) * SUBLANES, :]
                accs[r % 4] = accs[r % 4] + jnp.where(blk > cb, 1.0, 0.0)
            return tuple(accs)

        z = jnp.zeros((SUBLANES, QB), F32)
        a = lax.fori_loop(0, n2, body, (z, z, z, z))
        return jnp.sum((a[0] + a[1]) + (a[2] + a[3]), axis=0, keepdims=True)

    need = kf - count_gt(tau)

    ltri = jnp.where(row >= col, 1.0, 0.0).astype(BF16)

    def mask_chunk(c):
        blk = key_ref[chunk(c), :]
        eq = blk == tau
        pref = jnp.dot(ltri, jnp.where(eq, 1.0, 0.0).astype(BF16),
                       preferred_element_type=F32)
        carry = tie_ref[...]
        sel = (blk > tau) | (eq & (pref + carry <= need))
        sel = sel & (c * KC + row <= i * QB + col)
        tie_ref[...] = carry + pref[KC - 1:KC, :]
        return jnp.where(sel, jnp.inf, NEG_INF).T

    tie_ref[...] = jnp.zeros(tie_ref.shape, F32)

    aq = aq_ref[...]
    qs = jnp.concatenate([aq[:, j * LANES:(j + 1) * LANES] for j in range(NPAIR)], axis=0)
    mx_ref[...] = jnp.full(mx_ref.shape, -jnp.inf, F32)

    def logits_chunk(c2, bias_pos):
        mc = mask_chunk(c2)
        for r, g in ((0, G_KK0), (1, G_KK1)):
            lg = lax.dot_general(qs, kvi_ref[g, chunk(c2), :], NT_DIMS,
                                 preferred_element_type=F32)
            if bias_pos is not None:
                lg = lg + bias_ref[bias_pos, r]
            for j in range(NPAIR):
                rows = slice(j * QB, (j + 1) * QB)
                lj = jnp.minimum(lg[rows], mc)
                lg_ref[r, c2, rows, :] = lj
                mrows = slice(r * nrow + j * QB, r * nrow + (j + 1) * QB)
                mx_ref[mrows, :] = jnp.maximum(mx_ref[mrows, :],
                                               jnp.maximum(lj[:, :LANES], lj[:, LANES:]))

    _loop_quads(jnp.maximum(n2 - 2, 0), lambda c2: logits_chunk(c2, None))

    @pl.when(n2 >= 2)
    def _():
        logits_chunk(n2 - 2, 0)
        logits_chunk(n2 - 1, 1)

    @pl.when(n2 == 1)
    def _():
        logits_chunk(0, 1)

    mrow = jnp.max(mx_ref[...], axis=1, keepdims=True)
    acc_ref[...] = jnp.zeros(acc_ref.shape, F32)

    def pv_body(c2, carry):
        for r, g in ((0, G_VA), (1, G_VB)):
            p = jnp.exp2(lg_ref[r, c2] - mrow[r * nrow:(r + 1) * nrow])
            acc_ref[r * nrow:(r + 1) * nrow, :] += jnp.dot(
                p.astype(BF16), kvi_ref[g, chunk(c2), :], preferred_element_type=F32)
        return carry

    _loop_quads(n2, lambda c2: pv_body(c2, 0))

    lane = lax.broadcasted_iota(I32, (QB, LANES), 1)
    for j in range(NPAIR):
        oe = acc_ref[j * QB:(j + 1) * QB, :]
        oo = acc_ref[nrow + j * QB:nrow + (j + 1) * QB, :]
        out = jnp.where(lane < A_HEAD_DIM, oe / oe[:, A_HEAD_DIM:A_HEAD_DIM + 1], oo / oo[:, 0:1])
        out_ref[:, j * LANES:(j + 1) * LANES] = out.astype(BF16)


def _dsa(aq, iq, iw, kvi, bias, bsz, seq, k_top):
    t = aq.shape[0]
    nq = seq // QB
    return pl.pallas_call(
        functools.partial(_dsa_kernel, k_top=k_top),
        grid=(bsz, nq),
        in_specs=[
            pl.BlockSpec((QB, A_WIDTH), lambda b, i: (b * nq + i, 0)),
            pl.BlockSpec((QB, IDX_HEADS * IDX_DIM), lambda b, i: (b * nq + i, 0)),
            pl.BlockSpec((QB, LANES), lambda b, i: (b * nq + i, 0)),
            pl.BlockSpec((6, seq, LANES), lambda b, i: (0, b, 0)),
            pl.BlockSpec(bias.shape, lambda b, i: (0, 0, 0, 0)),
        ],
        out_specs=pl.BlockSpec((QB, A_WIDTH), lambda b, i: (b * nq + i, 0)),
        out_shape=jax.ShapeDtypeStruct((t, A_WIDTH), BF16),
        scratch_shapes=[
            pltpu.VMEM((seq, QB), I32),
            pltpu.VMEM((seq, QB), I16),
            pltpu.VMEM((seq, QB), I16),
            pltpu.VMEM((1, QB), F32),
            pltpu.VMEM((2, nq, NPAIR * QB, KC), F32),
            pltpu.VMEM((A_HEADS * QB, LANES), F32),
            pltpu.VMEM((A_HEADS * QB, LANES), F32),
        ],
        compiler_params=pltpu.CompilerParams(
            dimension_semantics=("arbitrary", "arbitrary"), vmem_limit_bytes=VMEM_LIMIT),
        name="dsa",
    )(aq, iq, iw, kvi, bias)


def _mla_kernel(q_ref, k_ref, v_ref, out_ref, lg_ref, mx_ref, ls_ref, acc_ref):
    tq = q_ref.shape[2]
    lane = lax.broadcasted_iota(I32, (tq, LANES), 1)

    def lane_fold(x, op):
        part = x[:, :LANES]
        for s in range(1, tq // LANES):
            part = op(part, x[:, s * LANES:(s + 1) * LANES])
        return part

    def logits(c, diag):
        for e in range(2):
            lg = lax.dot_general(q_ref[0, e], k_ref[0, e, c * tq:(c + 1) * tq, :], NT_DIMS,
                                 preferred_element_type=F32)
            if diag:
                row = lax.broadcasted_iota(I32, (tq, tq), 0)
                col = lax.broadcasted_iota(I32, (tq, tq), 1)
                lg = jnp.where(col <= row, lg, NEG_INF)
            lg_ref[e, c] = lg
            part = lane_fold(lg, jnp.maximum)
            mx_ref[e] = part if c == 0 else jnp.maximum(mx_ref[e], part)

    def pv(c, m):
        v = v_ref[0, 0, c * tq:(c + 1) * tq, :]
        for e in range(2):
            p = jnp.exp2(lg_ref[e, c] - m[e])
            part = lane_fold(p, jnp.add)
            ls_ref[e] = part if c == 0 else ls_ref[e] + part
            o = jnp.dot(p.astype(BF16), v, preferred_element_type=F32)
            acc_ref[e] = o if c == 0 else acc_ref[e] + o

    def attend(n):
        for c in range(n):
            logits(c, False)
        logits(n, True)
        m = [jnp.max(mx_ref[e], axis=1, keepdims=True) for e in range(2)]
        for c in range(n + 1):
            pv(c, m)
        outs = [acc_ref[e] / jnp.sum(ls_ref[e], axis=1, keepdims=True) for e in range(2)]
        out_ref[0] = jnp.where(lane < B_V_DIM, outs[0], outs[1]).astype(BF16)

    for n in range(k_ref.shape[2] // tq):
        pl.when(pl.program_id(2) == n)(functools.partial(attend, n))


def _mla(qm, km, vm, tq):
    bsz, _, seq, _ = qm.shape
    return pl.pallas_call(
        _mla_kernel,
        grid=(bsz, B_HEADS // 2, seq // tq),
        in_specs=[
            pl.BlockSpec((1, 2, tq, LANES), lambda b, p, i: (b, p, i, 0)),
            pl.BlockSpec((1, 2, seq, LANES), lambda b, p, i: (b, p, 0, 0)),
            pl.BlockSpec((1, 1, seq, LANES), lambda b, p, i: (b, p, 0, 0)),
        ],
        out_specs=pl.BlockSpec((1, tq, LANES), lambda b, p, i: (b, i, p)),
        out_shape=jax.ShapeDtypeStruct((bsz, seq, B_WIDTH), BF16),
        scratch_shapes=[
            pltpu.VMEM((2, seq // tq, tq, tq), F32),
            pltpu.VMEM((2, tq, LANES), F32),
            pltpu.VMEM((2, tq, LANES), F32),
            pltpu.VMEM((2, tq, LANES), F32),
        ],
        compiler_params=pltpu.CompilerParams(
            dimension_semantics=("arbitrary", "arbitrary", "arbitrary"),
            vmem_limit_bytes=VMEM_LIMIT),
        name="mla",
    )(qm, km, vm)


def _merge_kernel(x_ref, g_ref, wg_ref, bg_ref, ya_ref, yb_ref, wa_ref, wb_ref, wo_ref, o_ref):
    x = x_ref[...]
    hb = _rms(x, g_ref[...]).astype(BF16)
    gl = jnp.dot(hb, wg_ref[...], preferred_element_type=F32) + bg_ref[...]
    gates = 1.0 / (1.0 + jnp.exp(-gl))
    pa = jnp.dot(ya_ref[...], wa_ref[...], preferred_element_type=F32)
    pb = jnp.dot(yb_ref[...], wb_ref[...], preferred_element_type=F32)
    merged = gates[:, :D_MODEL] * pa + gates[:, D_MODEL:] * pb
    o_ref[...] = x + jnp.dot(merged.astype(BF16), wo_ref[...], preferred_element_type=F32)


def _merge(x2d, g, wg, bg, ya, yb, wa, wb, wo, tm):
    t = x2d.shape[0]
    const = lambda i: (0, 0)
    return pl.pallas_call(
        _merge_kernel,
        grid=(t // tm,),
        in_specs=[
            pl.BlockSpec((tm, D_MODEL), lambda i: (i, 0)),
            pl.BlockSpec((1, D_MODEL), const),
            pl.BlockSpec(wg.shape, const),
            pl.BlockSpec((1, N_BRANCHES * D_MODEL), const),
            pl.BlockSpec((tm, A_WIDTH), lambda i: (i, 0)),
            pl.BlockSpec((tm, B_WIDTH), lambda i: (i, 0)),
            pl.BlockSpec(wa.shape, const),
            pl.BlockSpec(wb.shape, const),
            pl.BlockSpec(wo.shape, const),
        ],
        out_specs=pl.BlockSpec((tm, D_MODEL), lambda i: (i, 0)),
        out_shape=jax.ShapeDtypeStruct((t, D_MODEL), F32),
        compiler_params=pltpu.CompilerParams(
            dimension_semantics=("arbitrary",), vmem_limit_bytes=VMEM_LIMIT),
        name="merge",
    )(x2d, g, wg, bg, ya, yb, wa, wb, wo)


def _mlp_kernel(x_ref, g_ref, w1_ref, w2_ref, gf_ref, o_ref, *, final, fc):
    x = x_ref[...]
    hb = _rms(x, g_ref[...]).astype(BF16)
    acc = jnp.zeros(x.shape, F32)
    for j in range(D_FF // fc):
        a = jnp.dot(hb, w1_ref[:, j * fc:(j + 1) * fc], preferred_element_type=F32)
        a = jnp.square(jnp.maximum(a, 0.0)).astype(BF16)
        acc = acc + jnp.dot(a, w2_ref[j * fc:(j + 1) * fc, :], preferred_element_type=F32)
    y = x + acc
    if final:
        y = _rms(y, gf_ref[...])
    o_ref[...] = y


def _mlp(x2d, g, w1, w2, gf, tm, final):
    t = x2d.shape[0]
    const = lambda i: (0, 0)
    return pl.pallas_call(
        functools.partial(_mlp_kernel, final=final, fc=1024),
        grid=(t // tm,),
        in_specs=[
            pl.BlockSpec((tm, D_MODEL), lambda i: (i, 0)),
            pl.BlockSpec((1, D_MODEL), const),
            pl.BlockSpec(w1.shape, const, pipeline_mode=pl.Buffered(1)),
            pl.BlockSpec(w2.shape, const, pipeline_mode=pl.Buffered(1)),
            pl.BlockSpec((1, D_MODEL), const),
        ],
        out_specs=pl.BlockSpec((tm, D_MODEL), lambda i: (i, 0)),
        out_shape=jax.ShapeDtypeStruct((t, D_MODEL), F32),
        compiler_params=pltpu.CompilerParams(
            dimension_semantics=("arbitrary",), vmem_limit_bytes=VMEM_LIMIT),
        name="mlp",
    )(x2d, g, w1, w2, gf)


def _split_w_in(w_in):
    cuts = [0]
    for s in IN_SIZES:
        cuts.append(cuts[-1] + s)
    return [w_in[..., cuts[n]:cuts[n + 1]] for n in range(len(IN_SIZES))]


def _prep_weights(w_in, w_uq, w_ukv):
    depth = w_in.shape[0]
    a_q, a_k, a_v, i_q, i_k, i_w, c_q, c_kv, k_r, w_g = _split_w_in(w_in)
    w1 = jnp.concatenate([a_q * (LOG2E * A_HEAD_DIM ** -0.5), i_q * IDX_DIM ** -0.5, c_q], axis=-1)
    w2 = jnp.concatenate([
        i_k, a_k, a_v, k_r, i_w * IDX_HEADS ** -0.5,
        jnp.zeros((depth, D_MODEL, LANES - IW_LANE - IDX_HEADS), F32), c_kv], axis=-1)
    assert w2.shape[-1] == W2_COLS
    wuq = w_uq.reshape(depth, Q_LORA, B_HEADS, B_QK_DIM)
    half = B_ROPE_DIM // 2
    wuq = jnp.concatenate([wuq, wuq[..., B_NOPE_DIM + half:], wuq[..., B_NOPE_DIM:B_NOPE_DIM + half]],
                          axis=-1)
    wuq = wuq.reshape(depth, Q_LORA, B_HEADS * LANES)
    wukv = w_ukv.reshape(depth, KV_LORA, B_HEADS, B_NOPE_DIM + B_V_DIM)
    wk = jnp.pad(wukv[..., :B_NOPE_DIM], ((0, 0), (0, 0), (0, 0), (0, LANES - B_NOPE_DIM)))
    wk = wk.reshape(depth, KV_LORA, B_HEADS * LANES)
    wv = wukv[..., B_NOPE_DIM:].reshape(depth, KV_LORA, B_HEADS * B_V_DIM)
    wukv = jnp.concatenate([wk, wv], axis=-1)
    return (w1.astype(BF16), w2.astype(BF16), w_g.astype(BF16), wuq.astype(BF16),
            wukv.astype(BF16))


def _rope_tables(positions):
    inv_freq = ROPE_THETA ** (-jnp.arange(0, B_ROPE_DIM, 2, dtype=F32) / B_ROPE_DIM)
    ang = positions.astype(F32).reshape(-1)[:, None] * inv_freq
    cos, sin = jnp.cos(ang), jnp.sin(ang)
    t = ang.shape[0]
    one = jnp.ones((t, B_NOPE_DIM), F32)
    z = lambda n: jnp.zeros((t, n), F32)
    cs = jnp.concatenate([one, cos, cos, z(LANES - B_QK_DIM)], axis=1)
    sn = jnp.concatenate([z(B_NOPE_DIM), -sin, sin, z(LANES - B_QK_DIM)], axis=1)
    return jnp.stack([cs, sn])


def _rel_bucket(dist):
    max_exact = REL_BUCKETS // 2
    n = jnp.maximum(dist, 0)
    nf = jnp.maximum(n.astype(F32), 1.0)
    log_b = max_exact + (jnp.log(nf / max_exact) / math.log(REL_MAX_DIST / max_exact)
                         * (REL_BUCKETS - max_exact)).astype(I32)
    return jnp.where(n < max_exact, n, jnp.minimum(log_b, REL_BUCKETS - 1))


def _bias_tables(rel_bias):
    rel = rel_bias - rel_bias[REL_BUCKETS - 1][None, :]
    q = jnp.arange(QB, dtype=I32)[:, None]
    s = jnp.arange(KC, dtype=I32)[None, :]

    def lookup(dist):
        bucket = _rel_bucket(dist)[..., None]
        out = jnp.zeros((QB, KC, A_HEADS), F32)
        for b in range(REL_BUCKETS - 1):
            out = jnp.where(bucket == b, rel[b][None, None, :], out)
        return out

    diag = jnp.where((q >= s)[..., None], lookup(q - s), 0.0)
    prev = lookup(KC + q - s)
    table = jnp.stack([prev, diag])
    table = table.reshape(2, QB, KC, NPAIR, 2)
    table = jnp.transpose(table, (0, 4, 3, 1, 2))
    return (table.reshape(2, 2, NPAIR * QB, KC) * LOG2E).astype(F32)


def kernel(x, positions, attn_norm_g, w_in, b_gate, q_latent_norm_g, kv_latent_norm_g, w_uq,
           w_ukv, w_branch_a, w_branch_b, w_out, mlp_norm_g, w_ff1, w_ff2, rel_bias,
           final_norm_g):
    bsz, seq, _ = x.shape
    depth = w_in.shape[0]
    t = bsz * seq
    k_top = min(TOPK_MAX, seq // 4)
    tm = min(512, seq)
    tq = min(512, seq)

    w1, w2, wg, wuq, wukv = _prep_weights(w_in, w_uq, w_ukv)
    wa, wb, wo = w_branch_a.astype(BF16), w_branch_b.astype(BF16), w_out.astype(BF16)
    wf1, wf2 = w_ff1.astype(BF16), w_ff2.astype(BF16)
    rope = _rope_tables(positions)
    bias = _bias_tables(rel_bias)
    gf = final_norm_g.reshape(1, D_MODEL)

    x2d = x.reshape(t, D_MODEL)
    for l in range(depth):
        g_attn = attn_norm_g[l].reshape(1, D_MODEL)
        aq, iq, kvi, iw, qm, km, vm = _proj(
            x2d, g_attn, w1[l], w2[l], q_latent_norm_g[l].reshape(1, Q_LORA),
            kv_latent_norm_g[l].reshape(1, KV_LORA), wuq[l], wukv[l], rope, bsz, seq, tm)
        ya = _dsa(aq, iq, iw, kvi, bias, bsz, seq, k_top)
        yb = _mla(qm, km, vm, tq).reshape(t, B_WIDTH)
        x2d = _merge(x2d, g_attn, wg[l], b_gate[l].reshape(1, N_BRANCHES * D_MODEL), ya, yb,
                     wa[l], wb[l], wo[l], tm)
        x2d = _mlp(x2d, mlp_norm_g[l].reshape(1, D_MODEL), wf1[l], wf2[l], gf, tm,
                   final=(l == depth - 1))
    return x2d.reshape(bsz, seq, D_MODEL)
```

```python
import functools
import math

import jax
import jax.numpy as jnp
from jax import lax
from jax.experimental import pallas as pl
from jax.experimental.pallas import tpu as pltpu

F32 = jnp.float32
BF16 = jnp.bfloat16
I32 = jnp.int32
I16 = jnp.int16

D_MODEL = 1024
A_HEADS = 8
A_HEAD_DIM = 64
A_WIDTH = A_HEADS * A_HEAD_DIM
IDX_HEADS = 4
IDX_DIM = 64
TOPK_MAX = 256
B_HEADS = 8
B_NOPE_DIM = 64
B_ROPE_DIM = 32
B_QK_DIM = B_NOPE_DIM + B_ROPE_DIM
B_V_DIM = 64
B_WIDTH = B_HEADS * B_V_DIM
Q_LORA = 256
KV_LORA = 128
ROPE_THETA = 10000.0
REL_BUCKETS = 32
REL_MAX_DIST = 128
N_BRANCHES = 2
D_FF = 4 * D_MODEL
EPS = 1e-6
NEG_INF = -1e30
LOG2E = math.log2(math.e)
IN_SIZES = (A_WIDTH, A_HEAD_DIM, A_HEAD_DIM, IDX_HEADS * IDX_DIM, IDX_DIM, IDX_HEADS,
            Q_LORA, KV_LORA, B_ROPE_DIM, N_BRANCHES * D_MODEL)

LANES = 128
SUBLANES = 8
QB = 2 * LANES
KC = QB
NPAIR = A_HEADS // 2
INT_MIN = -2 ** 31
I16_MIN = -2 ** 15
PACKED_ROWS = 2 * SUBLANES
G_IK0, G_IK1, G_KK0, G_KK1, G_VA, G_VB = range(6)
PROJ_SPLIT = 4
W2_COLS = 3 * LANES
IW_LANE = A_HEAD_DIM + B_ROPE_DIM
VMEM_LIMIT = 56 * 1024 * 1024

NT_DIMS = (((1,), (1,)), ((), ()))


def _rms(x, g):
    return x * lax.rsqrt(jnp.mean(x * x, axis=-1, keepdims=True) + EPS) * g


def _loop_quads(n, body):
    def quad(p, carry):
        for u in range(4):
            body(4 * p + u)
        return carry

    lax.fori_loop(0, n // 4, quad, 0)
    done = (n // 4) * 4

    @pl.when(n % 4 >= 2)
    def _():
        body(done)
        body(done + 1)

    @pl.when(n % 2 == 1)
    def _():
        body(n - 1)


def _proj_kernel(x_ref, g_ref, w1_ref, w2_ref, gq_ref, gkv_ref, wuq_ref, wukv_ref, rope_ref,
                 aq_ref, iq_ref, kvi_ref, iw_ref, qm_ref, km_ref, vm_ref):
    tm = x_ref.shape[0] // PROJ_SPLIT
    lane = lax.broadcasted_iota(I32, (tm, LANES), 1)
    low = lane < LANES // 2
    half = B_ROPE_DIM // 2
    for part in range(PROJ_SPLIT):
        rs = slice(part * tm, (part + 1) * tm)
        hb = _rms(x_ref[rs, :], g_ref[...]).astype(BF16)
        z1 = jnp.dot(hb, w1_ref[...], preferred_element_type=F32)
        z2 = jnp.dot(hb, w2_ref[...], preferred_element_type=F32)
        aq_ref[rs, :] = z1[:, :A_WIDTH].astype(BF16)
        iq_ref[rs, :] = z1[:, A_WIDTH:A_WIDTH + IDX_HEADS * IDX_DIM].astype(BF16)
        ga = z2[:, :LANES]
        gb = z2[:, LANES:2 * LANES]
        ra = pltpu.roll(ga, LANES // 2, 1)
        rb = pltpu.roll(gb, LANES // 2, 1)
        kvi_ref[G_IK0, rs, :] = jnp.where(low, ga, 0.0).astype(BF16)
        kvi_ref[G_IK1, rs, :] = jnp.where(low, 0.0, ra).astype(BF16)
        kvi_ref[G_KK0, rs, :] = jnp.where(low, ra, 0.0).astype(BF16)
        kvi_ref[G_KK1, rs, :] = jnp.where(low, 0.0, ga).astype(BF16)
        kvi_ref[G_VA, rs, :] = jnp.where(
            low, gb, jnp.where(lane == A_HEAD_DIM, 1.0, 0.0)).astype(BF16)
        kvi_ref[G_VB, rs, :] = jnp.where(low, jnp.where(lane == 0, 1.0, 0.0), rb).astype(BF16)
        iw_ref[rs, :] = gb

        cs, sn = rope_ref[0, rs, :], rope_ref[1, rs, :]
        cq = z1[:, A_WIDTH + IDX_HEADS * IDX_DIM:]
        qup = jnp.dot(_rms(cq, gq_ref[...]).astype(BF16), wuq_ref[...],
                      preferred_element_type=F32)
        scale = LOG2E * B_QK_DIM ** -0.5
        for h in range(B_HEADS):
            u = qup[:, h * LANES:(h + 1) * LANES]
            rot = u * cs + pltpu.roll(u, LANES - B_ROPE_DIM, 1) * sn
            qm_ref[0, h, rs, :] = (rot * scale).astype(BF16)
        ckv = z2[:, 2 * LANES:3 * LANES]
        kvup = jnp.dot(_rms(ckv, gkv_ref[...]).astype(BF16), wukv_ref[...],
                       preferred_element_type=F32)
        ku = jnp.where((lane >= B_NOPE_DIM) & (lane < B_QK_DIM), gb, 0.0)
        kr = (ku * cs
              + pltpu.roll(ku, half, 1) * jnp.where(lane >= B_NOPE_DIM + half, sn, 0.0)
              + pltpu.roll(ku, LANES - half, 1) * jnp.where(lane < B_NOPE_DIM + half, sn, 0.0))
        for h in range(B_HEADS):
            km_ref[0, h, rs, :] = (kvup[:, h * LANES:(h + 1) * LANES] + kr).astype(BF16)
        for p in range(B_HEADS // 2):
            c0 = B_HEADS * LANES + p * LANES
            vm_ref[0, p, rs, :] = kvup[:, c0:c0 + LANES].astype(BF16)


def _proj(x2d, g, w1, w2, gq, gkv, wuq, wukv, rope, bsz, seq, tm):
    t = x2d.shape[0]
    nt = seq // tm
    const = lambda i: (0, 0)
    return pl.pallas_call(
        _proj_kernel,
        grid=(t // tm,),
        in_specs=[
            pl.BlockSpec((tm, D_MODEL), lambda i: (i, 0)),
            pl.BlockSpec((1, D_MODEL), const),
            pl.BlockSpec(w1.shape, const),
            pl.BlockSpec(w2.shape, const),
            pl.BlockSpec((1, Q_LORA), const),
            pl.BlockSpec((1, KV_LORA), const),
            pl.BlockSpec(wuq.shape, const),
            pl.BlockSpec(wukv.shape, const),
            pl.BlockSpec((2, tm, LANES), lambda i: (0, i, 0)),
        ],
        out_specs=[
            pl.BlockSpec((tm, A_WIDTH), lambda i: (i, 0)),
            pl.BlockSpec((tm, IDX_HEADS * IDX_DIM), lambda i: (i, 0)),
            pl.BlockSpec((6, tm, LANES), lambda i: (0, i, 0)),
            pl.BlockSpec((tm, LANES), lambda i: (i, 0)),
            pl.BlockSpec((1, B_HEADS, tm, LANES), lambda i: (i // nt, 0, i % nt, 0)),
            pl.BlockSpec((1, B_HEADS, tm, LANES), lambda i: (i // nt, 0, i % nt, 0)),
            pl.BlockSpec((1, B_HEADS // 2, tm, LANES), lambda i: (i // nt, 0, i % nt, 0)),
        ],
        out_shape=[
            jax.ShapeDtypeStruct((t, A_WIDTH), BF16),
            jax.ShapeDtypeStruct((t, IDX_HEADS * IDX_DIM), BF16),
            jax.ShapeDtypeStruct((6, t, LANES), BF16),
            jax.ShapeDtypeStruct((t, LANES), F32),
            jax.ShapeDtypeStruct((bsz, B_HEADS, seq, LANES), BF16),
            jax.ShapeDtypeStruct((bsz, B_HEADS, seq, LANES), BF16),
            jax.ShapeDtypeStruct((bsz, B_HEADS // 2, seq, LANES), BF16),
        ],
        compiler_params=pltpu.CompilerParams(
            dimension_semantics=("arbitrary",), vmem_limit_bytes=VMEM_LIMIT),
        name="proj",
    )(x2d, g, w1, w2, gq, gkv, wuq, wukv, rope)


def _dsa_kernel(aq_ref, iq_ref, iw_ref, kvi_ref, bias_ref, out_ref,
                key_ref, hi_ref, lo_ref, tie_ref, lg_ref, mx_ref, acc_ref, *, k_top):
    i = pl.program_id(1)
    n2 = i + 1
    kf = float(k_top)
    nrow = NPAIR * QB

    row = lax.broadcasted_iota(I32, (KC, QB), 0)
    col = lax.broadcasted_iota(I32, (KC, QB), 1)

    def chunk(c):
        return pl.ds(pl.multiple_of(c * KC, KC), KC)

    iq = iq_ref[...]
    iqs = jnp.concatenate([iq[:, :LANES], iq[:, LANES:]], axis=0)
    iw_t = iw_ref[...].T
    w0, w1, w2, w3 = (iw_t[IW_LANE + h:IW_LANE + h + 1, :] for h in range(IDX_HEADS))

    def score_body(c, carry):
        t0 = lax.dot_general(kvi_ref[G_IK0, chunk(c), :], iqs, NT_DIMS,
                             preferred_element_type=F32)
        t1 = lax.dot_general(kvi_ref[G_IK1, chunk(c), :], iqs, NT_DIMS,
                             preferred_element_type=F32)
        s = (w0 * jnp.maximum(t0[:, :QB], 0.0) + w1 * jnp.maximum(t1[:, :QB], 0.0)
             + w2 * jnp.maximum(t0[:, QB:], 0.0) + w3 * jnp.maximum(t1[:, QB:], 0.0))
        b = lax.bitcast_convert_type(s, I32)
        sk = jnp.where(b < 0, b ^ 0x7FFFFFFF, b)
        sk = jnp.where(c * KC + row > i * QB + col, INT_MIN, sk)
        key_ref[chunk(c), :] = sk
        hi_ref[chunk(c), :] = (sk >> 16).astype(I16)
        lo_ref[chunk(c), :] = ((sk & 0xFFFF) + I16_MIN).astype(I16)
        return carry

    _loop_quads(n2, lambda c: score_body(c, 0))

    def count16(ref, cand):
        cb = jnp.broadcast_to(cand.astype(I16), (PACKED_ROWS, QB))

        def body(c, accs):
            keys = ref[chunk(c), :]
            accs = list(accs)
            for r in range(KC // PACKED_ROWS):
                blk = keys[r * PACKED_ROWS:(r + 1) * PACKED_ROWS, :]
                accs[r % 4] = accs[r % 4] + jnp.where(blk >= cb, jnp.int16(1), jnp.int16(0))
            return tuple(accs)

        z = jnp.zeros((PACKED_ROWS, QB), I16)
        a = lax.fori_loop(0, n2, body, (z, z, z, z))
        tot = (a[0] + a[1]) + (a[2] + a[3])
        return jnp.sum(tot.astype(F32), axis=0, keepdims=True)

    def bisect16(ref):
        def bit_body(it, base):
            cand = base + lax.shift_left(jnp.int32(1), 15 - it)
            return jnp.where(count16(ref, cand) >= kf, cand, base)

        return lax.fori_loop(0, 16, bit_body, jnp.full((1, QB), I16_MIN, I32))

    tau_hi = bisect16(hi_ref)
    th = tau_hi.astype(I16)

    def refine_body(c, carry):
        hi = hi_ref[chunk(c), :]
        lo_ref[chunk(c), :] = jnp.where(
            hi == th, lo_ref[chunk(c), :],
            jnp.where(hi > th, jnp.int16(-I16_MIN - 1), jnp.int16(I16_MIN)))
        return carry

    lax.fori_loop(0, n2, refine_body, 0)
    tau = tau_hi * 65536 + (bisect16(lo_ref) - I16_MIN)

    def count_gt(cand):
        cb = jnp.broadcast_to(cand, (SUBLANES, QB))

        def body(c, accs):
            keys = key_ref[chunk(c), :]
            accs = list(accs)
            for r in range(KC // SUBLANES):
                blk = keys[r * SUBLANES:(r + 1) * SUBLANES, :]
                accs[r % 4] = accs[r % 4] + jnp.where(blk > cb, 1.0, 0.0)
            return tuple(accs)

        z = jnp.zeros((SUBLANES, QB), F32)
        a = lax.fori_loop(0, n2, body, (z, z, z, z))
        return jnp.sum((a[0] + a[1]) + (a[2] + a[3]), axis=0, keepdims=True)

    need = kf - count_gt(tau)

    ltri = jnp.where(row >= col, 1.0, 0.0).astype(BF16)

    def mask_chunk(c):
        blk = key_ref[chunk(c), :]
        eq = blk == tau
        pref = jnp.dot(ltri, jnp.where(eq, 1.0, 0.0).astype(BF16),
                       preferred_element_type=F32)
        carry = tie_ref[...]
        sel = (blk > tau) | (eq & (pref + carry <= need))
        sel = sel & (c * KC + row <= i * QB + col)
        tie_ref[...] = carry + pref[KC - 1:KC, :]
        return jnp.where(sel, jnp.inf, NEG_INF).T

    tie_ref[...] = jnp.zeros(tie_ref.shape, F32)

    aq = aq_ref[...]
    qs = jnp.concatenate([aq[:, j * LANES:(j + 1) * LANES] for j in range(NPAIR)], axis=0)
    mx_ref[...] = jnp.full(mx_ref.shape, -jnp.inf, F32)

    def logits_chunk(c2, bias_pos):
        mc = mask_chunk(c2)
        for r, g in ((0, G_KK0), (1, G_KK1)):
            lg = lax.dot_general(qs, kvi_ref[g, chunk(c2), :], NT_DIMS,
                                 preferred_element_type=F32)
            if bias_pos is not None:
                lg = lg + bias_ref[bias_pos, r]
            for j in range(NPAIR):
                rows = slice(j * QB, (j + 1) * QB)
                lj = jnp.minimum(lg[rows], mc)
                lg_ref[r, c2, rows, :] = lj
                mrows = slice(r * nrow + j * QB, r * nrow + (j + 1) * QB)
                mx_ref[mrows, :] = jnp.maximum(mx_ref[mrows, :],
                                               jnp.maximum(lj[:, :LANES], lj[:, LANES:]))

    _loop_quads(jnp.maximum(n2 - 2, 0), lambda c2: logits_chunk(c2, None))

    @pl.when(n2 >= 2)
    def _():
        logits_chunk(n2 - 2, 0)
        logits_chunk(n2 - 1, 1)

    @pl.when(n2 == 1)
    def _():
        logits_chunk(0, 1)

    mrow = jnp.max(mx_ref[...], axis=1, keepdims=True)
    acc_ref[...] = jnp.zeros(acc_ref.shape, F32)

    def pv_body(c2, carry):
        for r, g in ((0, G_VA), (1, G_VB)):
            p = jnp.exp2(lg_ref[r, c2] - mrow[r * nrow:(r + 1) * nrow])
            acc_ref[r * nrow:(r + 1) * nrow, :] += jnp.dot(
                p.astype(BF16), kvi_ref[g, chunk(c2), :], preferred_element_type=F32)
        return carry

    _loop_quads(n2, lambda c2: pv_body(c2, 0))

    lane = lax.broadcasted_iota(I32, (QB, LANES), 1)
    for j in range(NPAIR):
        oe = acc_ref[j * QB:(j + 1) * QB, :]
        oo = acc_ref[nrow + j * QB:nrow + (j + 1) * QB, :]
        out = jnp.where(lane < A_HEAD_DIM, oe / oe[:, A_HEAD_DIM:A_HEAD_DIM + 1], oo / oo[:, 0:1])
        out_ref[:, j * LANES:(j + 1) * LANES] = out.astype(BF16)


def _dsa(aq, iq, iw, kvi, bias, bsz, seq, k_top):
    t = aq.shape[0]
    nq = seq // QB
    return pl.pallas_call(
        functools.partial(_dsa_kernel, k_top=k_top),
        grid=(bsz, nq),
        in_specs=[
            pl.BlockSpec((QB, A_WIDTH), lambda b, i: (b * nq + i, 0)),
            pl.BlockSpec((QB, IDX_HEADS * IDX_DIM), lambda b, i: (b * nq + i, 0)),
            pl.BlockSpec((QB, LANES), lambda b, i: (b * nq + i, 0)),
            pl.BlockSpec((6, seq, LANES), lambda b, i: (0, b, 0)),
            pl.BlockSpec(bias.shape, lambda b, i: (0, 0, 0, 0)),
        ],
        out_specs=pl.BlockSpec((QB, A_WIDTH), lambda b, i: (b * nq + i, 0)),
        out_shape=jax.ShapeDtypeStruct((t, A_WIDTH), BF16),
        scratch_shapes=[
            pltpu.VMEM((seq, QB), I32),
            pltpu.VMEM((seq, QB), I16),
            pltpu.VMEM((seq, QB), I16),
            pltpu.VMEM((1, QB), F32),
            pltpu.VMEM((2, nq, NPAIR * QB, KC), F32),
            pltpu.VMEM((A_HEADS * QB, LANES), F32),
            pltpu.VMEM((A_HEADS * QB, LANES), F32),
        ],
        compiler_params=pltpu.CompilerParams(
            dimension_semantics=("arbitrary", "arbitrary"), vmem_limit_bytes=VMEM_LIMIT),
        name="dsa",
    )(aq, iq, iw, kvi, bias)


def _mla_kernel(q_ref, k_ref, v_ref, out_ref, lg_ref, mx_ref, ls_ref, acc_ref):
    tq = q_ref.shape[2]
    lane = lax.broadcasted_iota(I32, (tq, LANES), 1)

    def lane_fold(x, op):
        part = x[:, :LANES]
        for s in range(1, x.shape[1] // LANES):
            part = op(part, x[:, s * LANES:(s + 1) * LANES])
        return part

    def logits(c, r0, r1, kw, diag):
        for e in range(2):
            lg = lax.dot_general(q_ref[0, e, r0:r1, :], k_ref[0, e, c * tq:c * tq + kw, :],
                                 NT_DIMS, preferred_element_type=F32)
            if diag:
                row = lax.broadcasted_iota(I32, (r1 - r0, kw), 0) + r0
                col = lax.broadcasted_iota(I32, (r1 - r0, kw), 1)
                lg = jnp.where(col <= row, lg, NEG_INF)
            lg_ref[e, c, r0:r1, :kw] = lg
            part = lane_fold(lg, jnp.maximum)
            mx_ref[e, r0:r1] = part if c == 0 else jnp.maximum(mx_ref[e, r0:r1], part)

    def pv(c, r0, r1, kw, m):
        v = v_ref[0, 0, c * tq:c * tq + kw, :]
        for e in range(2):
            p = jnp.exp2(lg_ref[e, c, r0:r1, :kw] - m[e][r0:r1])
            part = lane_fold(p, jnp.add)
            ls_ref[e, r0:r1] = part if c == 0 else ls_ref[e, r0:r1] + part
            o = jnp.dot(p.astype(BF16), v, preferred_element_type=F32)
            acc_ref[e, r0:r1] = o if c == 0 else acc_ref[e, r0:r1] + o

    def attend(n):
        spans = [(c, 0, tq, tq, False) for c in range(n)]
        spans += [(n, 0, tq // 2, tq // 2, True), (n, tq // 2, tq, tq, True)]
        for span in spans:
            logits(*span)
        m = [jnp.max(mx_ref[e], axis=1, keepdims=True) for e in range(2)]
        for c, r0, r1, kw, _ in spans:
            pv(c, r0, r1, kw, m)
        outs = [acc_ref[e] / jnp.sum(ls_ref[e], axis=1, keepdims=True) for e in range(2)]
        out_ref[0] = jnp.where(lane < B_V_DIM, outs[0], outs[1]).astype(BF16)

    for n in range(k_ref.shape[2] // tq):
        pl.when(pl.program_id(2) == n)(functools.partial(attend, n))


def _mla(qm, km, vm, tq):
    bsz, _, seq, _ = qm.shape
    return pl.pallas_call(
        _mla_kernel,
        grid=(bsz, B_HEADS // 2, seq // tq),
        in_specs=[
            pl.BlockSpec((1, 2, tq, LANES), lambda b, p, i: (b, p, i, 0)),
            pl.BlockSpec((1, 2, seq, LANES), lambda b, p, i: (b, p, 0, 0)),
            pl.BlockSpec((1, 1, seq, LANES), lambda b, p, i: (b, p, 0, 0)),
        ],
        out_specs=pl.BlockSpec((1, tq, LANES), lambda b, p, i: (b, i, p)),
        out_shape=jax.ShapeDtypeStruct((bsz, seq, B_WIDTH), BF16),
        scratch_shapes=[
            pltpu.VMEM((2, seq // tq, tq, tq), F32),
            pltpu.VMEM((2, tq, LANES), F32),
            pltpu.VMEM((2, tq, LANES), F32),
            pltpu.VMEM((2, tq, LANES), F32),
        ],
        compiler_params=pltpu.CompilerParams(
            dimension_semantics=("arbitrary", "arbitrary", "arbitrary"),
            vmem_limit_bytes=VMEM_LIMIT),
        name="mla",
    )(qm, km, vm)


def _merge_kernel(x_ref, g_ref, wg_ref, bg_ref, ya_ref, yb_ref, wa_ref, wb_ref, wo_ref, o_ref):
    x = x_ref[...]
    hb = _rms(x, g_ref[...]).astype(BF16)
    gl = jnp.dot(hb, wg_ref[...], preferred_element_type=F32) + bg_ref[...]
    gates = 1.0 / (1.0 + jnp.exp(-gl))
    pa = jnp.dot(ya_ref[...], wa_ref[...], preferred_element_type=F32)
    pb = jnp.dot(yb_ref[...], wb_ref[...], preferred_element_type=F32)
    merged = gates[:, :D_MODEL] * pa + gates[:, D_MODEL:] * pb
    o_ref[...] = x + jnp.dot(merged.astype(BF16), wo_ref[...], preferred_element_type=F32)


def _merge(x2d, g, wg, bg, ya, yb, wa, wb, wo, tm):
    t = x2d.shape[0]
    const = lambda i: (0, 0)
    return pl.pallas_call(
        _merge_kernel,
        grid=(t // tm,),
        in_specs=[
            pl.BlockSpec((tm, D_MODEL), lambda i: (i, 0)),
            pl.BlockSpec((1, D_MODEL), const),
            pl.BlockSpec(wg.shape, const),
            pl.BlockSpec((1, N_BRANCHES * D_MODEL), const),
            pl.BlockSpec((tm, A_WIDTH), lambda i: (i, 0)),
            pl.BlockSpec((tm, B_WIDTH), lambda i: (i, 0)),
            pl.BlockSpec(wa.shape, const),
            pl.BlockSpec(wb.shape, const),
            pl.BlockSpec(wo.shape, const),
        ],
        out_specs=pl.BlockSpec((tm, D_MODEL), lambda i: (i, 0)),
        out_shape=jax.ShapeDtypeStruct((t, D_MODEL), F32),
        compiler_params=pltpu.CompilerParams(
            dimension_semantics=("arbitrary",), vmem_limit_bytes=VMEM_LIMIT),
        name="merge",
    )(x2d, g, wg, bg, ya, yb, wa, wb, wo)


def _mlp_kernel(x_ref, g_ref, w1_ref, w2_ref, gf_ref, o_ref, *, final, fc):
    x = x_ref[...]
    hb = _rms(x, g_ref[...]).astype(BF16)
    acc = jnp.zeros(x.shape, F32)
    for j in range(D_FF // fc):
        a = jnp.dot(hb, w1_ref[:, j * fc:(j + 1) * fc], preferred_element_type=F32)
        a = jnp.square(jnp.maximum(a, 0.0)).astype(BF16)
        acc = acc + jnp.dot(a, w2_ref[j * fc:(j + 1) * fc, :], preferred_element_type=F32)
    y = x + acc
    if final:
        y = _rms(y, gf_ref[...])
    o_ref[...] = y


def _mlp(x2d, g, w1, w2, gf, tm, final):
    t = x2d.shape[0]
    const = lambda i: (0, 0)
    return pl.pallas_call(
        functools.partial(_mlp_kernel, final=final, fc=1024),
        grid=(t // tm,),
        in_specs=[
            pl.BlockSpec((tm, D_MODEL), lambda i: (i, 0)),
            pl.BlockSpec((1, D_MODEL), const),
            pl.BlockSpec(w1.shape, const, pipeline_mode=pl.Buffered(1)),
            pl.BlockSpec(w2.shape, const, pipeline_mode=pl.Buffered(1)),
            pl.BlockSpec((1, D_MODEL), const),
        ],
        out_specs=pl.BlockSpec((tm, D_MODEL), lambda i: (i, 0)),
        out_shape=jax.ShapeDtypeStruct((t, D_MODEL), F32),
        compiler_params=pltpu.CompilerParams(
            dimension_semantics=("arbitrary",), vmem_limit_bytes=VMEM_LIMIT),
        name="mlp",
    )(x2d, g, w1, w2, gf)


def _split_w_in(w_in):
    cuts = [0]
    for s in IN_SIZES:
        cuts.append(cuts[-1] + s)
    return [w_in[..., cuts[n]:cuts[n + 1]] for n in range(len(IN_SIZES))]


def _prep_weights(w_in, w_uq, w_ukv):
    depth = w_in.shape[0]
    a_q, a_k, a_v, i_q, i_k, i_w, c_q, c_kv, k_r, w_g = _split_w_in(w_in)
    w1 = jnp.concatenate([a_q * (LOG2E * A_HEAD_DIM ** -0.5), i_q * IDX_DIM ** -0.5, c_q], axis=-1)
    w2 = jnp.concatenate([
        i_k, a_k, a_v, k_r, i_w * IDX_HEADS ** -0.5,
        jnp.zeros((depth, D_MODEL, LANES - IW_LANE - IDX_HEADS), F32), c_kv], axis=-1)
    assert w2.shape[-1] == W2_COLS
    wuq = w_uq.reshape(depth, Q_LORA, B_HEADS, B_QK_DIM)
    half = B_ROPE_DIM // 2
    wuq = jnp.concatenate([wuq, wuq[..., B_NOPE_DIM + half:], wuq[..., B_NOPE_DIM:B_NOPE_DIM + half]],
                          axis=-1)
    wuq = wuq.reshape(depth, Q_LORA, B_HEADS * LANES)
    wukv = w_ukv.reshape(depth, KV_LORA, B_HEADS, B_NOPE_DIM + B_V_DIM)
    wk = jnp.pad(wukv[..., :B_NOPE_DIM], ((0, 0), (0, 0), (0, 0), (0, LANES - B_NOPE_DIM)))
    wk = wk.reshape(depth, KV_LORA, B_HEADS * LANES)
    wv = wukv[..., B_NOPE_DIM:].reshape(depth, KV_LORA, B_HEADS * B_V_DIM)
    wukv = jnp.concatenate([wk, wv], axis=-1)
    return (w1.astype(BF16), w2.astype(BF16), w_g.astype(BF16), wuq.astype(BF16),
            wukv.astype(BF16))


def _rope_tables(positions):
    inv_freq = ROPE_THETA ** (-jnp.arange(0, B_ROPE_DIM, 2, dtype=F32) / B_ROPE_DIM)
    ang = positions.astype(F32).reshape(-1)[:, None] * inv_freq
    cos, sin = jnp.cos(ang), jnp.sin(ang)
    t = ang.shape[0]
    one = jnp.ones((t, B_NOPE_DIM), F32)
    z = lambda n: jnp.zeros((t, n), F32)
    cs = jnp.concatenate([one, cos, cos, z(LANES - B_QK_DIM)], axis=1)
    sn = jnp.concatenate([z(B_NOPE_DIM), -sin, sin, z(LANES - B_QK_DIM)], axis=1)
    return jnp.stack([cs, sn])


def _rel_bucket(dist):
    max_exact = REL_BUCKETS // 2
    n = jnp.maximum(dist, 0)
    nf = jnp.maximum(n.astype(F32), 1.0)
    log_b = max_exact + (jnp.log(nf / max_exact) / math.log(REL_MAX_DIST / max_exact)
                         * (REL_BUCKETS - max_exact)).astype(I32)
    return jnp.where(n < max_exact, n, jnp.minimum(log_b, REL_BUCKETS - 1))


def _bias_tables(rel_bias):
    rel = rel_bias - rel_bias[REL_BUCKETS - 1][None, :]
    q = jnp.arange(QB, dtype=I32)[:, None]
    s = jnp.arange(KC, dtype=I32)[None, :]

    def lookup(dist):
        bucket = _rel_bucket(dist)[..., None]
        out = jnp.zeros((QB, KC, A_HEADS), F32)
        for b in range(REL_BUCKETS - 1):
            out = jnp.where(bucket == b, rel[b][None, None, :], out)
        return out

    diag = jnp.where((q >= s)[..., None], lookup(q - s), 0.0)
    prev = lookup(KC + q - s)
    table = jnp.stack([prev, diag])
    table = table.reshape(2, QB, KC, NPAIR, 2)
    table = jnp.transpose(table, (0, 4, 3, 1, 2))
    return (table.reshape(2, 2, NPAIR * QB, KC) * LOG2E).astype(F32)


def kernel(x, positions, attn_norm_g, w_in, b_gate, q_latent_norm_g, kv_latent_norm_g, w_uq,
           w_ukv, w_branch_a, w_branch_b, w_out, mlp_norm_g, w_ff1, w_ff2, rel_bias,
           final_norm_g):
    bsz, seq, _ = x.shape
    depth = w_in.shape[0]
    t = bsz * seq
    k_top = min(TOPK_MAX, seq // 4)
    tm = min(512, seq)
    tq = min(512, seq)

    w1, w2, wg, wuq, wukv = _prep_weights(w_in, w_uq, w_ukv)
    wa, wb, wo = w_branch_a.astype(BF16), w_branch_b.astype(BF16), w_out.astype(BF16)
    wf1, wf2 = w_ff1.astype(BF16), w_ff2.astype(BF16)
    rope = _rope_tables(positions)
    bias = _bias_tables(rel_bias)
    gf = final_norm_g.reshape(1, D_MODEL)

    x2d = x.reshape(t, D_MODEL)
    for l in range(depth):
        g_attn = attn_norm_g[l].reshape(1, D_MODEL)
        aq, iq, kvi, iw, qm, km, vm = _proj(
            x2d, g_attn, w1[l], w2[l], q_latent_norm_g[l].reshape(1, Q_LORA),
            kv_latent_norm_g[l].reshape(1, KV_LORA), wuq[l], wukv[l], rope, bsz, seq, tm)
        ya = _dsa(aq, iq, iw, kvi, bias, bsz, seq, k_top)
        yb = _mla(qm, km, vm, tq).reshape(t, B_WIDTH)
        x2d = _merge(x2d, g_attn, wg[l], b_gate[l].reshape(1, N_BRANCHES * D_MODEL), ya, yb,
                     wa[l], wb[l], wo[l], tm)
        x2d = _mlp(x2d, mlp_norm_g[l].reshape(1, D_MODEL), wf1[l], wf2[l], gf, tm,
                   final=(l == depth - 1))
    return x2d.reshape(bsz, seq, D_MODEL)
```

```python
import functools
import math

import jax
import jax.numpy as jnp
from jax import lax
from jax.experimental import pallas as pl
from jax.experimental.pallas import tpu as pltpu

F32 = jnp.float32
BF16 = jnp.bfloat16
I32 = jnp.int32
I16 = jnp.int16

D_MODEL = 1024
A_HEADS = 8
A_HEAD_DIM = 64
A_WIDTH = A_HEADS * A_HEAD_DIM
IDX_HEADS = 4
IDX_DIM = 64
TOPK_MAX = 256
B_HEADS = 8
B_NOPE_DIM = 64
B_ROPE_DIM = 32
B_QK_DIM = B_NOPE_DIM + B_ROPE_DIM
B_V_DIM = 64
B_WIDTH = B_HEADS * B_V_DIM
Q_LORA = 256
KV_LORA = 128
ROPE_THETA = 10000.0
REL_BUCKETS = 32
REL_MAX_DIST = 128
N_BRANCHES = 2
D_FF = 4 * D_MODEL
EPS = 1e-6
NEG_INF = -1e30
LOG2E = math.log2(math.e)
IN_SIZES = (A_WIDTH, A_HEAD_DIM, A_HEAD_DIM, IDX_HEADS * IDX_DIM, IDX_DIM, IDX_HEADS,
            Q_LORA, KV_LORA, B_ROPE_DIM, N_BRANCHES * D_MODEL)

LANES = 128
SUBLANES = 8
QB = 2 * LANES
KC = QB
NPAIR = A_HEADS // 2
INT_MIN = -2 ** 31
I16_MIN = -2 ** 15
PACKED_ROWS = 2 * SUBLANES
G_IK0, G_IK1, G_KK0, G_KK1, G_VA, G_VB = range(6)
PROJ_SPLIT = 4
W2_COLS = 3 * LANES
IW_LANE = A_HEAD_DIM + B_ROPE_DIM
VMEM_LIMIT = 56 * 1024 * 1024

NT_DIMS = (((1,), (1,)), ((), ()))


def _rms(x, g):
    return x * lax.rsqrt(jnp.mean(x * x, axis=-1, keepdims=True) + EPS) * g


def _loop_quads(n, body):
    def quad(p, carry):
        for u in range(4):
            body(4 * p + u)
        return carry

    lax.fori_loop(0, n // 4, quad, 0)
    done = (n // 4) * 4

    @pl.when(n % 4 >= 2)
    def _():
        body(done)
        body(done + 1)

    @pl.when(n % 2 == 1)
    def _():
        body(n - 1)


def _proj_kernel(x_ref, g_ref, w1_ref, w2_ref, gq_ref, gkv_ref, wuq_ref, wukv_ref, rope_ref,
                 aq_ref, iq_ref, kvi_ref, iw_ref, qm_ref, km_ref, vm_ref):
    tm = x_ref.shape[0] // PROJ_SPLIT
    lane = lax.broadcasted_iota(I32, (tm, LANES), 1)
    low = lane < LANES // 2
    half = B_ROPE_DIM // 2
    for part in range(PROJ_SPLIT):
        rs = slice(part * tm, (part + 1) * tm)
        hb = _rms(x_ref[rs, :], g_ref[...]).astype(BF16)
        z1 = jnp.dot(hb, w1_ref[...], preferred_element_type=F32)
        z2 = jnp.dot(hb, w2_ref[...], preferred_element_type=F32)
        aq_ref[rs, :] = z1[:, :A_WIDTH].astype(BF16)
        iq_ref[rs, :] = z1[:, A_WIDTH:A_WIDTH + IDX_HEADS * IDX_DIM].astype(BF16)
        ga = z2[:, :LANES]
        gb = z2[:, LANES:2 * LANES]
        ra = pltpu.roll(ga, LANES // 2, 1)
        rb = pltpu.roll(gb, LANES // 2, 1)
        kvi_ref[G_IK0, rs, :] = jnp.where(low, ga, 0.0).astype(BF16)
        kvi_ref[G_IK1, rs, :] = jnp.where(low, 0.0, ra).astype(BF16)
        kvi_ref[G_KK0, rs, :] = jnp.where(low, ra, 0.0).astype(BF16)
        kvi_ref[G_KK1, rs, :] = jnp.where(low, 0.0, ga).astype(BF16)
        kvi_ref[G_VA, rs, :] = jnp.where(
            low, gb, jnp.where(lane == A_HEAD_DIM, 1.0, 0.0)).astype(BF16)
        kvi_ref[G_VB, rs, :] = jnp.where(low, jnp.where(lane == 0, 1.0, 0.0), rb).astype(BF16)
        iw_ref[rs, :] = gb

        cs, sn = rope_ref[0, rs, :], rope_ref[1, rs, :]
        cq = z1[:, A_WIDTH + IDX_HEADS * IDX_DIM:]
        qup = jnp.dot(_rms(cq, gq_ref[...]).astype(BF16), wuq_ref[...],
                      preferred_element_type=F32)
        scale = LOG2E * B_QK_DIM ** -0.5
        for h in range(B_HEADS):
            u = qup[:, h * LANES:(h + 1) * LANES]
            rot = u * cs + pltpu.roll(u, LANES - B_ROPE_DIM, 1) * sn
            qm_ref[0, h, rs, :] = (rot * scale).astype(BF16)
        ckv = z2[:, 2 * LANES:3 * LANES]
        kvup = jnp.dot(_rms(ckv, gkv_ref[...]).astype(BF16), wukv_ref[...],
                       preferred_element_type=F32)
        ku = jnp.where((lane >= B_NOPE_DIM) & (lane < B_QK_DIM), gb, 0.0)
        kr = (ku * cs
              + pltpu.roll(ku, half, 1) * jnp.where(lane >= B_NOPE_DIM + half, sn, 0.0)
              + pltpu.roll(ku, LANES - half, 1) * jnp.where(lane < B_NOPE_DIM + half, sn, 0.0))
        for h in range(B_HEADS):
            km_ref[0, h, rs, :] = (kvup[:, h * LANES:(h + 1) * LANES] + kr).astype(BF16)
        for p in range(B_HEADS // 2):
            c0 = B_HEADS * LANES + p * LANES
            vm_ref[0, p, rs, :] = kvup[:, c0:c0 + LANES].astype(BF16)


def _proj(x2d, g, w1, w2, gq, gkv, wuq, wukv, rope, bsz, seq, tm):
    t = x2d.shape[0]
    nt = seq // tm
    const = lambda i: (0, 0)
    return pl.pallas_call(
        _proj_kernel,
        grid=(t // tm,),
        in_specs=[
            pl.BlockSpec((tm, D_MODEL), lambda i: (i, 0)),
            pl.BlockSpec((1, D_MODEL), const),
            pl.BlockSpec(w1.shape, const),
            pl.BlockSpec(w2.shape, const),
            pl.BlockSpec((1, Q_LORA), const),
            pl.BlockSpec((1, KV_LORA), const),
            pl.BlockSpec(wuq.shape, const),
            pl.BlockSpec(wukv.shape, const),
            pl.BlockSpec((2, tm, LANES), lambda i: (0, i, 0)),
        ],
        out_specs=[
            pl.BlockSpec((tm, A_WIDTH), lambda i: (i, 0)),
            pl.BlockSpec((tm, IDX_HEADS * IDX_DIM), lambda i: (i, 0)),
            pl.BlockSpec((6, tm, LANES), lambda i: (0, i, 0)),
            pl.BlockSpec((tm, LANES), lambda i: (i, 0)),
            pl.BlockSpec((1, B_HEADS, tm, LANES), lambda i: (i // nt, 0, i % nt, 0)),
            pl.BlockSpec((1, B_HEADS, tm, LANES), lambda i: (i // nt, 0, i % nt, 0)),
            pl.BlockSpec((1, B_HEADS // 2, tm, LANES), lambda i: (i // nt, 0, i % nt, 0)),
        ],
        out_shape=[
            jax.ShapeDtypeStruct((t, A_WIDTH), BF16),
            jax.ShapeDtypeStruct((t, IDX_HEADS * IDX_DIM), BF16),
            jax.ShapeDtypeStruct((6, t, LANES), BF16),
            jax.ShapeDtypeStruct((t, LANES), F32),
            jax.ShapeDtypeStruct((bsz, B_HEADS, seq, LANES), BF16),
            jax.ShapeDtypeStruct((bsz, B_HEADS, seq, LANES), BF16),
            jax.ShapeDtypeStruct((bsz, B_HEADS // 2, seq, LANES), BF16),
        ],
        compiler_params=pltpu.CompilerParams(
            dimension_semantics=("arbitrary",), vmem_limit_bytes=VMEM_LIMIT),
        name="proj",
    )(x2d, g, w1, w2, gq, gkv, wuq, wukv, rope)


def _dsa_kernel(aq_ref, iq_ref, iw_ref, kvi_ref, bias_ref, out_ref,
                key_ref, hi_ref, lo_ref, tie_ref, lg_ref, mx_ref, acc_ref, *, k_top):
    i = pl.program_id(1)
    n2 = i + 1
    kf = float(k_top)
    nrow = NPAIR * QB

    row = lax.broadcasted_iota(I32, (KC, QB), 0)
    col = lax.broadcasted_iota(I32, (KC, QB), 1)

    def chunk(c):
        return pl.ds(pl.multiple_of(c * KC, KC), KC)

    iq = iq_ref[...]
    iqs = jnp.concatenate([iq[:, :LANES], iq[:, LANES:]], axis=0)
    iw_t = iw_ref[...].T
    w0, w1, w2, w3 = (iw_t[IW_LANE + h:IW_LANE + h + 1, :] for h in range(IDX_HEADS))

    def score_body(c, carry):
        t0 = lax.dot_general(kvi_ref[G_IK0, chunk(c), :], iqs, NT_DIMS,
                             preferred_element_type=F32)
        t1 = lax.dot_general(kvi_ref[G_IK1, chunk(c), :], iqs, NT_DIMS,
                             preferred_element_type=F32)
        s = (w0 * jnp.maximum(t0[:, :QB], 0.0) + w1 * jnp.maximum(t1[:, :QB], 0.0)
             + w2 * jnp.maximum(t0[:, QB:], 0.0) + w3 * jnp.maximum(t1[:, QB:], 0.0))
        b = lax.bitcast_convert_type(s, I32)
        sk = jnp.where(b < 0, b ^ 0x7FFFFFFF, b)
        sk = jnp.where(c * KC + row > i * QB + col, INT_MIN, sk)
        key_ref[chunk(c), :] = sk
        hi_ref[chunk(c), :] = (sk >> 16).astype(I16)
        lo_ref[chunk(c), :] = ((sk & 0xFFFF) + I16_MIN).astype(I16)
        return carry

    _loop_quads(n2, lambda c: score_body(c, 0))

    def count16(ref, cand):
        cb = jnp.broadcast_to(cand.astype(I16), (PACKED_ROWS, QB))

        def body(c, accs):
            keys = ref[chunk(c), :]
            accs = list(accs)
            for r in range(KC // PACKED_ROWS):
                blk = keys[r * PACKED_ROWS:(r + 1) * PACKED_ROWS, :]
                accs[r % 4] = accs[r % 4] + jnp.where(blk >= cb, jnp.int16(1), jnp.int16(0))
            return tuple(accs)

        z = jnp.zeros((PACKED_ROWS, QB), I16)
        a = lax.fori_loop(0, n2, body, (z, z, z, z))
        tot = (a[0] + a[1]) + (a[2] + a[3])
        return jnp.sum(tot.astype(F32), axis=0, keepdims=True)

    def bisect16(ref, cnt0):
        def bit_body(it, carry):
            base, cnt_base = carry
            cand = base + lax.shift_left(jnp.int32(1), 15 - it)
            cnt = count16(ref, cand)
            ok = cnt >= kf
            return jnp.where(ok, cand, base), jnp.where(ok, cnt, cnt_base)

        return lax.fori_loop(0, 16, bit_body, (jnp.full((1, QB), I16_MIN, I32), cnt0))

    tau_hi, cnt_hi = bisect16(hi_ref, jnp.zeros((1, QB), F32) + (n2 * KC).astype(F32))
    th = tau_hi.astype(I16)

    def refine_body(c, carry):
        hi = hi_ref[chunk(c), :]
        lo_ref[chunk(c), :] = jnp.where(
            hi == th, lo_ref[chunk(c), :],
            jnp.where(hi > th, jnp.int16(-I16_MIN - 1), jnp.int16(I16_MIN)))
        return carry

    lax.fori_loop(0, n2, refine_body, 0)
    tau_lo, cnt_ge = bisect16(lo_ref, cnt_hi)
    tau = tau_hi * 65536 + (tau_lo - I16_MIN)

    def count_gt(cand):
        cb = jnp.broadcast_to(cand, (SUBLANES, QB))

        def body(c, accs):
            keys = key_ref[chunk(c), :]
            accs = list(accs)
            for r in range(KC // SUBLANES):
                blk = keys[r * SUBLANES:(r + 1) * SUBLANES, :]
                accs[r % 4] = accs[r % 4] + jnp.where(blk > cb, 1.0, 0.0)
            return tuple(accs)

        z = jnp.zeros((SUBLANES, QB), F32)
        a = lax.fori_loop(0, n2, body, (z, z, z, z))
        return jnp.sum((a[0] + a[1]) + (a[2] + a[3]), axis=0, keepdims=True)

    def mask_all(c):
        sel = (key_ref[chunk(c), :] >= tau) & (c * KC + row <= i * QB + col)
        return jnp.where(sel, jnp.inf, NEG_INF).T

    def mask_ties(c, need, ltri):
        blk = key_ref[chunk(c), :]
        eq = blk == tau
        pref = jnp.dot(ltri, jnp.where(eq, 1.0, 0.0).astype(BF16),
                       preferred_element_type=F32)
        carry = tie_ref[...]
        sel = (blk > tau) | (eq & (pref + carry <= need))
        sel = sel & (c * KC + row <= i * QB + col)
        tie_ref[...] = carry + pref[KC - 1:KC, :]
        return jnp.where(sel, jnp.inf, NEG_INF).T

    aq = aq_ref[...]
    qs = jnp.concatenate([aq[:, j * LANES:(j + 1) * LANES] for j in range(NPAIR)], axis=0)
    mx_ref[...] = jnp.full(mx_ref.shape, -jnp.inf, F32)

    def logits_chunk(mask_chunk, c2, bias_pos):
        mc = mask_chunk(c2)
        for r, g in ((0, G_KK0), (1, G_KK1)):
            lg = lax.dot_general(qs, kvi_ref[g, chunk(c2), :], NT_DIMS,
                                 preferred_element_type=F32)
            if bias_pos is not None:
                lg = lg + bias_ref[bias_pos, r]
            for j in range(NPAIR):
                rows = slice(j * QB, (j + 1) * QB)
                lj = jnp.minimum(lg[rows], mc)
                lg_ref[r, c2, rows, :] = lj
                mrows = slice(r * nrow + j * QB, r * nrow + (j + 1) * QB)
                mx_ref[mrows, :] = jnp.maximum(mx_ref[mrows, :],
                                               jnp.maximum(lj[:, :LANES], lj[:, LANES:]))

    def all_logits(mask_chunk):
        _loop_quads(jnp.maximum(n2 - 2, 0), lambda c2: logits_chunk(mask_chunk, c2, None))

        @pl.when(n2 >= 2)
        def _():
            logits_chunk(mask_chunk, n2 - 2, 0)
            logits_chunk(mask_chunk, n2 - 1, 1)

        @pl.when(n2 == 1)
        def _():
            logits_chunk(mask_chunk, 0, 1)

    surplus = jnp.max(cnt_ge) > kf

    @pl.when(surplus)
    def _():
        need = kf - count_gt(tau)
        ltri = jnp.where(row >= col, 1.0, 0.0).astype(BF16)
        tie_ref[...] = jnp.zeros(tie_ref.shape, F32)
        all_logits(lambda c: mask_ties(c, need, ltri))

    @pl.when(jnp.logical_not(surplus))
    def _():
        all_logits(mask_all)

    mrow = jnp.max(mx_ref[...], axis=1, keepdims=True)
    acc_ref[...] = jnp.zeros(acc_ref.shape, F32)

    def pv_body(c2, carry):
        for r, g in ((0, G_VA), (1, G_VB)):
            p = jnp.exp2(lg_ref[r, c2] - mrow[r * nrow:(r + 1) * nrow])
            acc_ref[r * nrow:(r + 1) * nrow, :] += jnp.dot(
                p.astype(BF16), kvi_ref[g, chunk(c2), :], preferred_element_type=F32)
        return carry

    _loop_quads(n2, lambda c2: pv_body(c2, 0))

    lane = lax.broadcasted_iota(I32, (QB, LANES), 1)
    for j in range(NPAIR):
        oe = acc_ref[j * QB:(j + 1) * QB, :]
        oo = acc_ref[nrow + j * QB:nrow + (j + 1) * QB, :]
        out = jnp.where(lane < A_HEAD_DIM, oe / oe[:, A_HEAD_DIM:A_HEAD_DIM + 1], oo / oo[:, 0:1])
        out_ref[:, j * LANES:(j + 1) * LANES] = out.astype(BF16)


def _dsa(aq, iq, iw, kvi, bias, bsz, seq, k_top):
    t = aq.shape[0]
    nq = seq // QB
    return pl.pallas_call(
        functools.partial(_dsa_kernel, k_top=k_top),
        grid=(bsz, nq),
        in_specs=[
            pl.BlockSpec((QB, A_WIDTH), lambda b, i: (b * nq + i, 0)),
            pl.BlockSpec((QB, IDX_HEADS * IDX_DIM), lambda b, i: (b * nq + i, 0)),
            pl.BlockSpec((QB, LANES), lambda b, i: (b * nq + i, 0)),
            pl.BlockSpec((6, seq, LANES), lambda b, i: (0, b, 0)),
            pl.BlockSpec(bias.shape, lambda b, i: (0, 0, 0, 0)),
        ],
        out_specs=pl.BlockSpec((QB, A_WIDTH), lambda b, i: (b * nq + i, 0)),
        out_shape=jax.ShapeDtypeStruct((t, A_WIDTH), BF16),
        scratch_shapes=[
            pltpu.VMEM((seq, QB), I32),
            pltpu.VMEM((seq, QB), I16),
            pltpu.VMEM((seq, QB), I16),
            pltpu.VMEM((1, QB), F32),
            pltpu.VMEM((2, nq, NPAIR * QB, KC), F32),
            pltpu.VMEM((A_HEADS * QB, LANES), F32),
            pltpu.VMEM((A_HEADS * QB, LANES), F32),
        ],
        compiler_params=pltpu.CompilerParams(
            dimension_semantics=("arbitrary", "arbitrary"), vmem_limit_bytes=VMEM_LIMIT),
        name="dsa",
    )(aq, iq, iw, kvi, bias)


def _mla_kernel(q_ref, k_ref, v_ref, out_ref, lg_ref, mx_ref, ls_ref, acc_ref):
    tq = q_ref.shape[2]
    lane = lax.broadcasted_iota(I32, (tq, LANES), 1)

    def lane_fold(x, op):
        part = x[:, :LANES]
        for s in range(1, x.shape[1] // LANES):
            part = op(part, x[:, s * LANES:(s + 1) * LANES])
        return part

    def logits(c, r0, r1, kw, diag):
        for e in range(2):
            lg = lax.dot_general(q_ref[0, e, r0:r1, :], k_ref[0, e, c * tq:c * tq + kw, :],
                                 NT_DIMS, preferred_element_type=F32)
            if diag:
                row = lax.broadcasted_iota(I32, (r1 - r0, kw), 0) + r0
                col = lax.broadcasted_iota(I32, (r1 - r0, kw), 1)
                lg = jnp.where(col <= row, lg, NEG_INF)
            lg_ref[e, c, r0:r1, :kw] = lg
            part = lane_fold(lg, jnp.maximum)
            mx_ref[e, r0:r1] = part if c == 0 else jnp.maximum(mx_ref[e, r0:r1], part)

    def pv(c, r0, r1, kw, m):
        v = v_ref[0, 0, c * tq:c * tq + kw, :]
        for e in range(2):
            p = jnp.exp2(lg_ref[e, c, r0:r1, :kw] - m[e][r0:r1])
            part = lane_fold(p, jnp.add)
            ls_ref[e, r0:r1] = part if c == 0 else ls_ref[e, r0:r1] + part
            o = jnp.dot(p.astype(BF16), v, preferred_element_type=F32)
            acc_ref[e, r0:r1] = o if c == 0 else acc_ref[e, r0:r1] + o

    def attend(n):
        spans = [(c, 0, tq, tq, False) for c in range(n)]
        spans += [(n, 0, tq // 2, tq // 2, True), (n, tq // 2, tq, tq, True)]
        for span in spans:
            logits(*span)
        m = [jnp.max(mx_ref[e], axis=1, keepdims=True) for e in range(2)]
        for c, r0, r1, kw, _ in spans:
            pv(c, r0, r1, kw, m)
        outs = [acc_ref[e] / jnp.sum(ls_ref[e], axis=1, keepdims=True) for e in range(2)]
        out_ref[0] = jnp.where(lane < B_V_DIM, outs[0], outs[1]).astype(BF16)

    for n in range(k_ref.shape[2] // tq):
        pl.when(pl.program_id(2) == n)(functools.partial(attend, n))


def _mla(qm, km, vm, tq):
    bsz, _, seq, _ = qm.shape
    return pl.pallas_call(
        _mla_kernel,
        grid=(bsz, B_HEADS // 2, seq // tq),
        in_specs=[
            pl.BlockSpec((1, 2, tq, LANES), lambda b, p, i: (b, p, i, 0)),
            pl.BlockSpec((1, 2, seq, LANES), lambda b, p, i: (b, p, 0, 0)),
            pl.BlockSpec((1, 1, seq, LANES), lambda b, p, i: (b, p, 0, 0)),
        ],
        out_specs=pl.BlockSpec((1, tq, LANES), lambda b, p, i: (b, i, p)),
        out_shape=jax.ShapeDtypeStruct((bsz, seq, B_WIDTH), BF16),
        scratch_shapes=[
            pltpu.VMEM((2, seq // tq, tq, tq), F32),
            pltpu.VMEM((2, tq, LANES), F32),
            pltpu.VMEM((2, tq, LANES), F32),
            pltpu.VMEM((2, tq, LANES), F32),
        ],
        compiler_params=pltpu.CompilerParams(
            dimension_semantics=("arbitrary", "arbitrary", "arbitrary"),
            vmem_limit_bytes=VMEM_LIMIT),
        name="mla",
    )(qm, km, vm)


def _merge_kernel(x_ref, g_ref, wg_ref, bg_ref, ya_ref, yb_ref, wa_ref, wb_ref, wo_ref, o_ref):
    x = x_ref[...]
    hb = _rms(x, g_ref[...]).astype(BF16)
    gl = jnp.dot(hb, wg_ref[...], preferred_element_type=F32) + bg_ref[...]
    gates = 1.0 / (1.0 + jnp.exp(-gl))
    pa = jnp.dot(ya_ref[...], wa_ref[...], preferred_element_type=F32)
    pb = jnp.dot(yb_ref[...], wb_ref[...], preferred_element_type=F32)
    merged = gates[:, :D_MODEL] * pa + gates[:, D_MODEL:] * pb
    o_ref[...] = x + jnp.dot(merged.astype(BF16), wo_ref[...], preferred_element_type=F32)


def _merge(x2d, g, wg, bg, ya, yb, wa, wb, wo, tm):
    t = x2d.shape[0]
    const = lambda i: (0, 0)
    return pl.pallas_call(
        _merge_kernel,
        grid=(t // tm,),
        in_specs=[
            pl.BlockSpec((tm, D_MODEL), lambda i: (i, 0)),
            pl.BlockSpec((1, D_MODEL), const),
            pl.BlockSpec(wg.shape, const),
            pl.BlockSpec((1, N_BRANCHES * D_MODEL), const),
            pl.BlockSpec((tm, A_WIDTH), lambda i: (i, 0)),
            pl.BlockSpec((tm, B_WIDTH), lambda i: (i, 0)),
            pl.BlockSpec(wa.shape, const),
            pl.BlockSpec(wb.shape, const),
            pl.BlockSpec(wo.shape, const),
        ],
        out_specs=pl.BlockSpec((tm, D_MODEL), lambda i: (i, 0)),
        out_shape=jax.ShapeDtypeStruct((t, D_MODEL), F32),
        compiler_params=pltpu.CompilerParams(
            dimension_semantics=("arbitrary",), vmem_limit_bytes=VMEM_LIMIT),
        name="merge",
    )(x2d, g, wg, bg, ya, yb, wa, wb, wo)


def _mlp_kernel(x_ref, g_ref, w1_ref, w2_ref, gf_ref, o_ref, *, final, fc):
    x = x_ref[...]
    hb = _rms(x, g_ref[...]).astype(BF16)
    acc = jnp.zeros(x.shape, F32)
    for j in range(D_FF // fc):
        a = jnp.dot(hb, w1_ref[:, j * fc:(j + 1) * fc], preferred_element_type=F32)
        a = jnp.square(jnp.maximum(a, 0.0)).astype(BF16)
        acc = acc + jnp.dot(a, w2_ref[j * fc:(j + 1) * fc, :], preferred_element_type=F32)
    y = x + acc
    if final:
        y = _rms(y, gf_ref[...])
    o_ref[...] = y


def _mlp(x2d, g, w1, w2, gf, tm, final):
    t = x2d.shape[0]
    const = lambda i: (0, 0)
    return pl.pallas_call(
        functools.partial(_mlp_kernel, final=final, fc=1024),
        grid=(t // tm,),
        in_specs=[
            pl.BlockSpec((tm, D_MODEL), lambda i: (i, 0)),
            pl.BlockSpec((1, D_MODEL), const),
            pl.BlockSpec(w1.shape, const, pipeline_mode=pl.Buffered(1)),
            pl.BlockSpec(w2.shape, const, pipeline_mode=pl.Buffered(1)),
            pl.BlockSpec((1, D_MODEL), const),
        ],
        out_specs=pl.BlockSpec((tm, D_MODEL), lambda i: (i, 0)),
        out_shape=jax.ShapeDtypeStruct((t, D_MODEL), F32),
        compiler_params=pltpu.CompilerParams(
            dimension_semantics=("arbitrary",), vmem_limit_bytes=VMEM_LIMIT),
        name="mlp",
    )(x2d, g, w1, w2, gf)


def _split_w_in(w_in):
    cuts = [0]
    for s in IN_SIZES:
        cuts.append(cuts[-1] + s)
    return [w_in[..., cuts[n]:cuts[n + 1]] for n in range(len(IN_SIZES))]


def _prep_weights(w_in, w_uq, w_ukv):
    depth = w_in.shape[0]
    a_q, a_k, a_v, i_q, i_k, i_w, c_q, c_kv, k_r, w_g = _split_w_in(w_in)
    w1 = jnp.concatenate([a_q * (LOG2E * A_HEAD_DIM ** -0.5), i_q * IDX_DIM ** -0.5, c_q], axis=-1)
    w2 = jnp.concatenate([
        i_k, a_k, a_v, k_r, i_w * IDX_HEADS ** -0.5,
        jnp.zeros((depth, D_MODEL, LANES - IW_LANE - IDX_HEADS), F32), c_kv], axis=-1)
    assert w2.shape[-1] == W2_COLS
    wuq = w_uq.reshape(depth, Q_LORA, B_HEADS, B_QK_DIM)
    half = B_ROPE_DIM // 2
    wuq = jnp.concatenate([wuq, wuq[..., B_NOPE_DIM + half:], wuq[..., B_NOPE_DIM:B_NOPE_DIM + half]],
                          axis=-1)
    wuq = wuq.reshape(depth, Q_LORA, B_HEADS * LANES)
    wukv = w_ukv.reshape(depth, KV_LORA, B_HEADS, B_NOPE_DIM + B_V_DIM)
    wk = jnp.pad(wukv[..., :B_NOPE_DIM], ((0, 0), (0, 0), (0, 0), (0, LANES - B_NOPE_DIM)))
    wk = wk.reshape(depth, KV_LORA, B_HEADS * LANES)
    wv = wukv[..., B_NOPE_DIM:].reshape(depth, KV_LORA, B_HEADS * B_V_DIM)
    wukv = jnp.concatenate([wk, wv], axis=-1)
    return (w1.astype(BF16), w2.astype(BF16), w_g.astype(BF16), wuq.astype(BF16),
            wukv.astype(BF16))


def _rope_tables(positions):
    inv_freq = ROPE_THETA ** (-jnp.arange(0, B_ROPE_DIM, 2, dtype=F32) / B_ROPE_DIM)
    ang = positions.astype(F32).reshape(-1)[:, None] * inv_freq
    cos, sin = jnp.cos(ang), jnp.sin(ang)
    t = ang.shape[0]
    one = jnp.ones((t, B_NOPE_DIM), F32)
    z = lambda n: jnp.zeros((t, n), F32)
    cs = jnp.concatenate([one, cos, cos, z(LANES - B_QK_DIM)], axis=1)
    sn = jnp.concatenate([z(B_NOPE_DIM), -sin, sin, z(LANES - B_QK_DIM)], axis=1)
    return jnp.stack([cs, sn])


def _rel_bucket(dist):
    max_exact = REL_BUCKETS // 2
    n = jnp.maximum(dist, 0)
    nf = jnp.maximum(n.astype(F32), 1.0)
    log_b = max_exact + (jnp.log(nf / max_exact) / math.log(REL_MAX_DIST / max_exact)
                         * (REL_BUCKETS - max_exact)).astype(I32)
    return jnp.where(n < max_exact, n, jnp.minimum(log_b, REL_BUCKETS - 1))


def _bias_tables(rel_bias):
    rel = rel_bias - rel_bias[REL_BUCKETS - 1][None, :]
    q = jnp.arange(QB, dtype=I32)[:, None]
    s = jnp.arange(KC, dtype=I32)[None, :]

    def lookup(dist):
        bucket = _rel_bucket(dist)[..., None]
        out = jnp.zeros((QB, KC, A_HEADS), F32)
        for b in range(REL_BUCKETS - 1):
            out = jnp.where(bucket == b, rel[b][None, None, :], out)
        return out

    diag = jnp.where((q >= s)[..., None], lookup(q - s), 0.0)
    prev = lookup(KC + q - s)
    table = jnp.stack([prev, diag])
    table = table.reshape(2, QB, KC, NPAIR, 2)
    table = jnp.transpose(table, (0, 4, 3, 1, 2))
    return (table.reshape(2, 2, NPAIR * QB, KC) * LOG2E).astype(F32)


def kernel(x, positions, attn_norm_g, w_in, b_gate, q_latent_norm_g, kv_latent_norm_g, w_uq,
           w_ukv, w_branch_a, w_branch_b, w_out, mlp_norm_g, w_ff1, w_ff2, rel_bias,
           final_norm_g):
    bsz, seq, _ = x.shape
    depth = w_in.shape[0]
    t = bsz * seq
    k_top = min(TOPK_MAX, seq // 4)
    tm = min(512, seq)
    tq = min(512, seq)

    w1, w2, wg, wuq, wukv = _prep_weights(w_in, w_uq, w_ukv)
    wa, wb, wo = w_branch_a.astype(BF16), w_branch_b.astype(BF16), w_out.astype(BF16)
    wf1, wf2 = w_ff1.astype(BF16), w_ff2.astype(BF16)
    rope = _rope_tables(positions)
    bias = _bias_tables(rel_bias)
    gf = final_norm_g.reshape(1, D_MODEL)

    x2d = x.reshape(t, D_MODEL)
    for l in range(depth):
        g_attn = attn_norm_g[l].reshape(1, D_MODEL)
        aq, iq, kvi, iw, qm, km, vm = _proj(
            x2d, g_attn, w1[l], w2[l], q_latent_norm_g[l].reshape(1, Q_LORA),
            kv_latent_norm_g[l].reshape(1, KV_LORA), wuq[l], wukv[l], rope, bsz, seq, tm)
        ya = _dsa(aq, iq, iw, kvi, bias, bsz, seq, k_top)
        yb = _mla(qm, km, vm, tq).reshape(t, B_WIDTH)
        x2d = _merge(x2d, g_attn, wg[l], b_gate[l].reshape(1, N_BRANCHES * D_MODEL), ya, yb,
                     wa[l], wb[l], wo[l], tm)
        x2d = _mlp(x2d, mlp_norm_g[l].reshape(1, D_MODEL), wf1[l], wf2[l], gf, tm,
                   final=(l == depth - 1))
    return x2d.reshape(bsz, seq, D_MODEL)
```

```python
import functools
import math

import jax
import jax.numpy as jnp
from jax import lax
from jax.experimental import pallas as pl
from jax.experimental.pallas import tpu as pltpu

F32 = jnp.float32
BF16 = jnp.bfloat16
I32 = jnp.int32
I16 = jnp.int16

D_MODEL = 1024
A_HEADS = 8
A_HEAD_DIM = 64
A_WIDTH = A_HEADS * A_HEAD_DIM
IDX_HEADS = 4
IDX_DIM = 64
TOPK_MAX = 256
B_HEADS = 8
B_NOPE_DIM = 64
B_ROPE_DIM = 32
B_QK_DIM = B_NOPE_DIM + B_ROPE_DIM
B_V_DIM = 64
B_WIDTH = B_HEADS * B_V_DIM
Q_LORA = 256
KV_LORA = 128
ROPE_THETA = 10000.0
REL_BUCKETS = 32
REL_MAX_DIST = 128
N_BRANCHES = 2
D_FF = 4 * D_MODEL
EPS = 1e-6
NEG_INF = -1e30
LOG2E = math.log2(math.e)
IN_SIZES = (A_WIDTH, A_HEAD_DIM, A_HEAD_DIM, IDX_HEADS * IDX_DIM, IDX_DIM, IDX_HEADS,
            Q_LORA, KV_LORA, B_ROPE_DIM, N_BRANCHES * D_MODEL)

LANES = 128
SUBLANES = 8
QB = 2 * LANES
KC = QB
NPAIR = A_HEADS // 2
INT_MIN = -2 ** 31
I16_MIN = -2 ** 15
PACKED_ROWS = 2 * SUBLANES
G_IK0, G_IK1, G_KK0, G_KK1, G_VA, G_VB = range(6)
PROJ_SPLIT = 4
W2_COLS = 3 * LANES
IW_LANE = A_HEAD_DIM + B_ROPE_DIM
VMEM_LIMIT = 56 * 1024 * 1024

NT_DIMS = (((1,), (1,)), ((), ()))


def _rms(x, g):
    return x * lax.rsqrt(jnp.mean(x * x, axis=-1, keepdims=True) + EPS) * g


def _loop_quads(n, body):
    def quad(p, carry):
        for u in range(4):
            body(4 * p + u)
        return carry

    lax.fori_loop(0, n // 4, quad, 0)
    done = (n // 4) * 4

    @pl.when(n % 4 >= 2)
    def _():
        body(done)
        body(done + 1)

    @pl.when(n % 2 == 1)
    def _():
        body(n - 1)


def _proj_kernel(x_ref, g_ref, w1_ref, w2_ref, gq_ref, gkv_ref, wuq_ref, wukv_ref, rope_ref,
                 aq_ref, iq_ref, kvi_ref, iw_ref, qm_ref, km_ref, vm_ref):
    tm = x_ref.shape[0] // PROJ_SPLIT
    lane = lax.broadcasted_iota(I32, (tm, LANES), 1)
    low = lane < LANES // 2
    half = B_ROPE_DIM // 2
    for part in range(PROJ_SPLIT):
        rs = slice(part * tm, (part + 1) * tm)
        hb = _rms(x_ref[rs, :], g_ref[...]).astype(BF16)
        z1 = jnp.dot(hb, w1_ref[...], preferred_element_type=F32)
        z2 = jnp.dot(hb, w2_ref[...], preferred_element_type=F32)
        aq_ref[rs, :] = z1[:, :A_WIDTH].astype(BF16)
        iq_ref[rs, :] = z1[:, A_WIDTH:A_WIDTH + IDX_HEADS * IDX_DIM].astype(BF16)
        ga = z2[:, :LANES]
        gb = z2[:, LANES:2 * LANES]
        ra = pltpu.roll(ga, LANES // 2, 1)
        rb = pltpu.roll(gb, LANES // 2, 1)
        kvi_ref[G_IK0, rs, :] = jnp.where(low, ga, 0.0).astype(BF16)
        kvi_ref[G_IK1, rs, :] = jnp.where(low, 0.0, ra).astype(BF16)
        kvi_ref[G_KK0, rs, :] = jnp.where(low, ra, 0.0).astype(BF16)
        kvi_ref[G_KK1, rs, :] = jnp.where(low, 0.0, ga).astype(BF16)
        kvi_ref[G_VA, rs, :] = jnp.where(
            low, gb, jnp.where(lane == A_HEAD_DIM, 1.0, 0.0)).astype(BF16)
        kvi_ref[G_VB, rs, :] = jnp.where(low, jnp.where(lane == 0, 1.0, 0.0), rb).astype(BF16)
        iw_ref[rs, :] = gb

        cs, sn = rope_ref[0, rs, :], rope_ref[1, rs, :]
        cq = z1[:, A_WIDTH + IDX_HEADS * IDX_DIM:]
        qup = jnp.dot(_rms(cq, gq_ref[...]).astype(BF16), wuq_ref[...],
                      preferred_element_type=F32)
        scale = LOG2E * B_QK_DIM ** -0.5
        for h in range(B_HEADS):
            u = qup[:, h * LANES:(h + 1) * LANES]
            rot = u * cs + pltpu.roll(u, LANES - B_ROPE_DIM, 1) * sn
            qm_ref[0, h, rs, :] = (rot * scale).astype(BF16)
        ckv = z2[:, 2 * LANES:3 * LANES]
        kvup = jnp.dot(_rms(ckv, gkv_ref[...]).astype(BF16), wukv_ref[...],
                       preferred_element_type=F32)
        ku = jnp.where((lane >= B_NOPE_DIM) & (lane < B_QK_DIM), gb, 0.0)
        kr = (ku * cs
              + pltpu.roll(ku, half, 1) * jnp.where(lane >= B_NOPE_DIM + half, sn, 0.0)
              + pltpu.roll(ku, LANES - half, 1) * jnp.where(lane < B_NOPE_DIM + half, sn, 0.0))
        for h in range(B_HEADS):
            km_ref[0, h, rs, :] = (kvup[:, h * LANES:(h + 1) * LANES] + kr).astype(BF16)
        for p in range(B_HEADS // 2):
            c0 = B_HEADS * LANES + p * LANES
            vm_ref[0, p, rs, :] = kvup[:, c0:c0 + LANES].astype(BF16)


def _proj(x2d, g, w1, w2, gq, gkv, wuq, wukv, rope, bsz, seq, tm):
    t = x2d.shape[0]
    nt = seq // tm
    const = lambda i: (0, 0)
    return pl.pallas_call(
        _proj_kernel,
        grid=(t // tm,),
        in_specs=[
            pl.BlockSpec((tm, D_MODEL), lambda i: (i, 0)),
            pl.BlockSpec((1, D_MODEL), const),
            pl.BlockSpec(w1.shape, const),
            pl.BlockSpec(w2.shape, const),
            pl.BlockSpec((1, Q_LORA), const),
            pl.BlockSpec((1, KV_LORA), const),
            pl.BlockSpec(wuq.shape, const),
            pl.BlockSpec(wukv.shape, const),
            pl.BlockSpec((2, tm, LANES), lambda i: (0, i, 0)),
        ],
        out_specs=[
            pl.BlockSpec((tm, A_WIDTH), lambda i: (i, 0)),
            pl.BlockSpec((tm, IDX_HEADS * IDX_DIM), lambda i: (i, 0)),
            pl.BlockSpec((6, tm, LANES), lambda i: (0, i, 0)),
            pl.BlockSpec((tm, LANES), lambda i: (i, 0)),
            pl.BlockSpec((1, B_HEADS, tm, LANES), lambda i: (i // nt, 0, i % nt, 0)),
            pl.BlockSpec((1, B_HEADS, tm, LANES), lambda i: (i // nt, 0, i % nt, 0)),
            pl.BlockSpec((1, B_HEADS // 2, tm, LANES), lambda i: (i // nt, 0, i % nt, 0)),
        ],
        out_shape=[
            jax.ShapeDtypeStruct((t, A_WIDTH), BF16),
            jax.ShapeDtypeStruct((t, IDX_HEADS * IDX_DIM), BF16),
            jax.ShapeDtypeStruct((6, t, LANES), BF16),
            jax.ShapeDtypeStruct((t, LANES), F32),
            jax.ShapeDtypeStruct((bsz, B_HEADS, seq, LANES), BF16),
            jax.ShapeDtypeStruct((bsz, B_HEADS, seq, LANES), BF16),
            jax.ShapeDtypeStruct((bsz, B_HEADS // 2, seq, LANES), BF16),
        ],
        compiler_params=pltpu.CompilerParams(
            dimension_semantics=("arbitrary",), vmem_limit_bytes=VMEM_LIMIT),
        name="proj",
    )(x2d, g, w1, w2, gq, gkv, wuq, wukv, rope)


def _dsa_kernel(aq_ref, iq_ref, iw_ref, kvi_ref, bias_ref, out_ref,
                key_ref, hi_ref, lo_ref, tie_ref, lg_ref, mx_ref, acc_ref, *, k_top):
    i = pl.program_id(1)
    n2 = i + 1
    kf = float(k_top)
    nrow = NPAIR * QB

    row = lax.broadcasted_iota(I32, (KC, QB), 0)
    col = lax.broadcasted_iota(I32, (KC, QB), 1)

    def chunk(c):
        return pl.ds(pl.multiple_of(c * KC, KC), KC)

    iq = iq_ref[...]
    iqs = jnp.concatenate([iq[:, :LANES], iq[:, LANES:]], axis=0)
    iw_t = iw_ref[...].T
    w0, w1, w2, w3 = (iw_t[IW_LANE + h:IW_LANE + h + 1, :] for h in range(IDX_HEADS))

    def score_body(c, carry):
        t0 = lax.dot_general(kvi_ref[G_IK0, chunk(c), :], iqs, NT_DIMS,
                             preferred_element_type=F32)
        t1 = lax.dot_general(kvi_ref[G_IK1, chunk(c), :], iqs, NT_DIMS,
                             preferred_element_type=F32)
        s = (w0 * jnp.maximum(t0[:, :QB], 0.0) + w1 * jnp.maximum(t1[:, :QB], 0.0)
             + w2 * jnp.maximum(t0[:, QB:], 0.0) + w3 * jnp.maximum(t1[:, QB:], 0.0))
        b = lax.bitcast_convert_type(s, I32)
        sk = jnp.where(b < 0, b ^ 0x7FFFFFFF, b)
        sk = jnp.where(c * KC + row > i * QB + col, INT_MIN, sk)
        key_ref[chunk(c), :] = sk
        hi_ref[chunk(c), :] = (sk >> 16).astype(I16)
        lo_ref[chunk(c), :] = ((sk & 0xFFFF) + I16_MIN).astype(I16)
        return carry

    _loop_quads(n2, lambda c: score_body(c, 0))

    def count16(ref, cand):
        cb = jnp.broadcast_to(cand.astype(I16), (PACKED_ROWS, QB))

        def body(c, accs):
            keys = ref[chunk(c), :]
            accs = list(accs)
            for r in range(KC // PACKED_ROWS):
                blk = keys[r * PACKED_ROWS:(r + 1) * PACKED_ROWS, :]
                accs[r % 4] = accs[r % 4] + jnp.where(blk >= cb, jnp.int16(1), jnp.int16(0))
            return tuple(accs)

        z = jnp.zeros((PACKED_ROWS, QB), I16)
        a = lax.fori_loop(0, n2, body, (z, z, z, z))
        tot = (a[0] + a[1]) + (a[2] + a[3])
        return jnp.sum(tot.astype(F32), axis=0, keepdims=True)

    def bisect16(ref):
        def bit_body(it, base):
            cand = base + lax.shift_left(jnp.int32(1), 15 - it)
            return jnp.where(count16(ref, cand) >= kf, cand, base)

        return lax.fori_loop(0, 16, bit_body, jnp.full((1, QB), I16_MIN, I32))

    tau_hi = bisect16(hi_ref)
    th = tau_hi.astype(I16)

    def refine_body(c, carry):
        hi = hi_ref[chunk(c), :]
        lo_ref[chunk(c), :] = jnp.where(
            hi == th, lo_ref[chunk(c), :],
            jnp.where(hi > th, jnp.int16(-I16_MIN - 1), jnp.int16(I16_MIN)))
        return carry

    lax.fori_loop(0, n2, refine_body, 0)
    tau = tau_hi * 65536 + (bisect16(lo_ref) - I16_MIN)

    def count_gt(cand):
        cb = jnp.broadcast_to(cand, (SUBLANES, QB))

        def body(c, accs):
            keys = key_ref[chunk(c), :]
            accs = list(accs)
            for r in range(KC // SUBLANES):
                blk = keys[r * SUBLANES:(r + 1) * SUBLANES, :]
                accs[r % 4] = accs[r % 4] + jnp.where(blk > cb, 1.0, 0.0)
            return tuple(accs)

        z = jnp.zeros((SUBLANES, QB), F32)
        a = lax.fori_loop(0, n2, body, (z, z, z, z))
        return jnp.sum((a[0] + a[1]) + (a[2] + a[3]), axis=0, keepdims=True)

    need = kf - count_gt(tau)

    ltri = jnp.where(row >= col, 1.0, 0.0).astype(BF16)

    def mask_chunk(c):
        blk = key_ref[chunk(c), :]
        eq = blk == tau
        pref = jnp.dot(ltri, jnp.where(eq, 1.0, 0.0).astype(BF16),
                       preferred_element_type=F32)
        carry = tie_ref[...]
        sel = (blk > tau) | (eq & (pref + carry <= need))
        sel = sel & (c * KC + row <= i * QB + col)
        tie_ref[...] = carry + pref[KC - 1:KC, :]
        return jnp.where(sel, jnp.inf, NEG_INF).T

    tie_ref[...] = jnp.zeros(tie_ref.shape, F32)

    aq = aq_ref[...]
    qs = jnp.concatenate([aq[:, j * LANES:(j + 1) * LANES] for j in range(NPAIR)], axis=0)
    mx_ref[...] = jnp.full(mx_ref.shape, -jnp.inf, F32)

    def logits_chunk(c2, bias_pos):
        mc = mask_chunk(c2)
        for r, g in ((0, G_KK0), (1, G_KK1)):
            lg = lax.dot_general(qs, kvi_ref[g, chunk(c2), :], NT_DIMS,
                                 preferred_element_type=F32)
            if bias_pos is not None:
                lg = lg + bias_ref[bias_pos, r]
            for j in range(NPAIR):
                rows = slice(j * QB, (j + 1) * QB)
                lj = jnp.minimum(lg[rows], mc)
                lg_ref[r, c2, rows, :] = lj
                mrows = slice(r * nrow + j * QB, r * nrow + (j + 1) * QB)
                mx_ref[mrows, :] = jnp.maximum(mx_ref[mrows, :],
                                               jnp.maximum(lj[:, :LANES], lj[:, LANES:]))

    _loop_quads(jnp.maximum(n2 - 2, 0), lambda c2: logits_chunk(c2, None))

    @pl.when(n2 >= 2)
    def _():
        logits_chunk(n2 - 2, 0)
        logits_chunk(n2 - 1, 1)

    @pl.when(n2 == 1)
    def _():
        logits_chunk(0, 1)

    mrow = jnp.max(mx_ref[...], axis=1, keepdims=True)
    acc_ref[...] = jnp.zeros(acc_ref.shape, F32)

    def pv_body(c2, carry):
        for r, g in ((0, G_VA), (1, G_VB)):
            p = jnp.exp2(lg_ref[r, c2] - mrow[r * nrow:(r + 1) * nrow])
            acc_ref[r * nrow:(r + 1) * nrow, :] += jnp.dot(
                p.astype(BF16), kvi_ref[g, chunk(c2), :], preferred_element_type=F32)
        return carry

    _loop_quads(n2, lambda c2: pv_body(c2, 0))

    lane = lax.broadcasted_iota(I32, (QB, LANES), 1)
    for j in range(NPAIR):
        oe = acc_ref[j * QB:(j + 1) * QB, :]
        oo = acc_ref[nrow + j * QB:nrow + (j + 1) * QB, :]
        out = jnp.where(lane < A_HEAD_DIM, oe / oe[:, A_HEAD_DIM:A_HEAD_DIM + 1], oo / oo[:, 0:1])
        out_ref[:, j * LANES:(j + 1) * LANES] = out.astype(BF16)


def _dsa(aq, iq, iw, kvi, bias, bsz, seq, k_top):
    t = aq.shape[0]
    nq = seq // QB
    return pl.pallas_call(
        functools.partial(_dsa_kernel, k_top=k_top),
        grid=(bsz, nq),
        in_specs=[
            pl.BlockSpec((QB, A_WIDTH), lambda b, i: (b * nq + i, 0)),
            pl.BlockSpec((QB, IDX_HEADS * IDX_DIM), lambda b, i: (b * nq + i, 0)),
            pl.BlockSpec((QB, LANES), lambda b, i: (b * nq + i, 0)),
            pl.BlockSpec((6, seq, LANES), lambda b, i: (0, b, 0)),
            pl.BlockSpec(bias.shape, lambda b, i: (0, 0, 0, 0)),
        ],
        out_specs=pl.BlockSpec((QB, A_WIDTH), lambda b, i: (b * nq + i, 0)),
        out_shape=jax.ShapeDtypeStruct((t, A_WIDTH), BF16),
        scratch_shapes=[
            pltpu.VMEM((seq, QB), I32),
            pltpu.VMEM((seq, QB), I16),
            pltpu.VMEM((seq, QB), I16),
            pltpu.VMEM((1, QB), F32),
            pltpu.VMEM((2, nq, NPAIR * QB, KC), F32),
            pltpu.VMEM((A_HEADS * QB, LANES), F32),
            pltpu.VMEM((A_HEADS * QB, LANES), F32),
        ],
        compiler_params=pltpu.CompilerParams(
            dimension_semantics=("arbitrary", "arbitrary"), vmem_limit_bytes=VMEM_LIMIT),
        name="dsa",
    )(aq, iq, iw, kvi, bias)


def _mla_kernel(q_ref, k_ref, v_ref, out_ref, lg_ref, mx_ref, ls_ref, acc_ref):
    tq = q_ref.shape[2]
    lane = lax.broadcasted_iota(I32, (tq, LANES), 1)

    def lane_fold(x, op):
        part = x[:, :LANES]
        for s in range(1, x.shape[1] // LANES):
            part = op(part, x[:, s * LANES:(s + 1) * LANES])
        return part

    def logits(c, r0, r1, kw, diag):
        for e in range(2):
            lg = lax.dot_general(q_ref[0, e, r0:r1, :], k_ref[0, e, c * tq:c * tq + kw, :],
                                 NT_DIMS, preferred_element_type=F32)
            if diag:
                row = lax.broadcasted_iota(I32, (r1 - r0, kw), 0) + r0
                col = lax.broadcasted_iota(I32, (r1 - r0, kw), 1)
                lg = jnp.where(col <= row, lg, NEG_INF)
            lg_ref[e, c, r0:r1, :kw] = lg
            part = lane_fold(lg, jnp.maximum)
            mx_ref[e, r0:r1] = part if c == 0 else jnp.maximum(mx_ref[e, r0:r1], part)

    def pv(c, r0, r1, kw, m):
        v = v_ref[0, 0, c * tq:c * tq + kw, :]
        for e in range(2):
            p = jnp.exp2(lg_ref[e, c, r0:r1, :kw] - m[e][r0:r1])
            part = lane_fold(p, jnp.add)
            ls_ref[e, r0:r1] = part if c == 0 else ls_ref[e, r0:r1] + part
            o = jnp.dot(p.astype(BF16), v, preferred_element_type=F32)
            acc_ref[e, r0:r1] = o if c == 0 else acc_ref[e, r0:r1] + o

    def attend(n):
        spans = [(c, 0, tq, tq, False) for c in range(n)]
        spans += [(n, 0, tq // 2, tq // 2, True), (n, tq // 2, tq, tq, True)]
        for span in spans:
            logits(*span)
        m = [jnp.max(mx_ref[e], axis=1, keepdims=True) for e in range(2)]
        for c, r0, r1, kw, _ in spans:
            pv(c, r0, r1, kw, m)
        outs = [acc_ref[e] / jnp.sum(ls_ref[e], axis=1, keepdims=True) for e in range(2)]
        out_ref[0] = jnp.where(lane < B_V_DIM, outs[0], outs[1]).astype(BF16)

    for n in range(k_ref.shape[2] // tq):
        pl.when(pl.program_id(0) == n)(functools.partial(attend, n))


def _mla(qm, km, vm, tq):
    bsz, _, seq, _ = qm.shape
    return pl.pallas_call(
        _mla_kernel,
        grid=(seq // tq, bsz, B_HEADS // 2),
        in_specs=[
            pl.BlockSpec((1, 2, tq, LANES), lambda i, b, p: (b, p, i, 0)),
            pl.BlockSpec((1, 2, seq, LANES), lambda i, b, p: (b, p, 0, 0)),
            pl.BlockSpec((1, 1, seq, LANES), lambda i, b, p: (b, p, 0, 0)),
        ],
        out_specs=pl.BlockSpec((1, tq, LANES), lambda i, b, p: (b, i, p)),
        out_shape=jax.ShapeDtypeStruct((bsz, seq, B_WIDTH), BF16),
        scratch_shapes=[
            pltpu.VMEM((2, seq // tq, tq, tq), F32),
            pltpu.VMEM((2, tq, LANES), F32),
            pltpu.VMEM((2, tq, LANES), F32),
            pltpu.VMEM((2, tq, LANES), F32),
        ],
        compiler_params=pltpu.CompilerParams(
            dimension_semantics=("arbitrary", "arbitrary", "arbitrary"),
            vmem_limit_bytes=VMEM_LIMIT),
        name="mla",
    )(qm, km, vm)


def _merge_kernel(x_ref, g_ref, wg_ref, bg_ref, ya_ref, yb_ref, wa_ref, wb_ref, wo_ref, o_ref):
    x = x_ref[...]
    hb = _rms(x, g_ref[...]).astype(BF16)
    gl = jnp.dot(hb, wg_ref[...], preferred_element_type=F32) + bg_ref[...]
    gates = 1.0 / (1.0 + jnp.exp(-gl))
    pa = jnp.dot(ya_ref[...], wa_ref[...], preferred_element_type=F32)
    pb = jnp.dot(yb_ref[...], wb_ref[...], preferred_element_type=F32)
    merged = gates[:, :D_MODEL] * pa + gates[:, D_MODEL:] * pb
    o_ref[...] = x + jnp.dot(merged.astype(BF16), wo_ref[...], preferred_element_type=F32)


def _merge(x2d, g, wg, bg, ya, yb, wa, wb, wo, tm):
    t = x2d.shape[0]
    const = lambda i: (0, 0)
    return pl.pallas_call(
        _merge_kernel,
        grid=(t // tm,),
        in_specs=[
            pl.BlockSpec((tm, D_MODEL), lambda i: (i, 0)),
            pl.BlockSpec((1, D_MODEL), const),
            pl.BlockSpec(wg.shape, const),
            pl.BlockSpec((1, N_BRANCHES * D_MODEL), const),
            pl.BlockSpec((tm, A_WIDTH), lambda i: (i, 0)),
            pl.BlockSpec((tm, B_WIDTH), lambda i: (i, 0)),
            pl.BlockSpec(wa.shape, const),
            pl.BlockSpec(wb.shape, const),
            pl.BlockSpec(wo.shape, const),
        ],
        out_specs=pl.BlockSpec((tm, D_MODEL), lambda i: (i, 0)),
        out_shape=jax.ShapeDtypeStruct((t, D_MODEL), F32),
        compiler_params=pltpu.CompilerParams(
            dimension_semantics=("arbitrary",), vmem_limit_bytes=VMEM_LIMIT),
        name="merge",
    )(x2d, g, wg, bg, ya, yb, wa, wb, wo)


def _mlp_kernel(x_ref, g_ref, w1_ref, w2_ref, gf_ref, o_ref, *, final, fc):
    x = x_ref[...]
    hb = _rms(x, g_ref[...]).astype(BF16)
    acc = jnp.zeros(x.shape, F32)
    for j in range(D_FF // fc):
        a = jnp.dot(hb, w1_ref[:, j * fc:(j + 1) * fc], preferred_element_type=F32)
        a = jnp.square(jnp.maximum(a, 0.0)).astype(BF16)
        acc = acc + jnp.dot(a, w2_ref[j * fc:(j + 1) * fc, :], preferred_element_type=F32)
    y = x + acc
    if final:
        y = _rms(y, gf_ref[...])
    o_ref[...] = y


def _mlp(x2d, g, w1, w2, gf, tm, final):
    t = x2d.shape[0]
    const = lambda i: (0, 0)
    return pl.pallas_call(
        functools.partial(_mlp_kernel, final=final, fc=1024),
        grid=(t // tm,),
        in_specs=[
            pl.BlockSpec((tm, D_MODEL), lambda i: (i, 0)),
            pl.BlockSpec((1, D_MODEL), const),
            pl.BlockSpec(w1.shape, const, pipeline_mode=pl.Buffered(1)),
            pl.BlockSpec(w2.shape, const, pipeline_mode=pl.Buffered(1)),
            pl.BlockSpec((1, D_MODEL), const),
        ],
        out_specs=pl.BlockSpec((tm, D_MODEL), lambda i: (i, 0)),
        out_shape=jax.ShapeDtypeStruct((t, D_MODEL), F32),
        compiler_params=pltpu.CompilerParams(
            dimension_semantics=("arbitrary",), vmem_limit_bytes=VMEM_LIMIT),
        name="mlp",
    )(x2d, g, w1, w2, gf)


def _split_w_in(w_in):
    cuts = [0]
    for s in IN_SIZES:
        cuts.append(cuts[-1] + s)
    return [w_in[..., cuts[n]:cuts[n + 1]] for n in range(len(IN_SIZES))]


def _prep_weights(w_in, w_uq, w_ukv):
    depth = w_in.shape[0]
    a_q, a_k, a_v, i_q, i_k, i_w, c_q, c_kv, k_r, w_g = _split_w_in(w_in)
    w1 = jnp.concatenate([a_q * (LOG2E * A_HEAD_DIM ** -0.5), i_q * IDX_DIM ** -0.5, c_q], axis=-1)
    w2 = jnp.concatenate([
        i_k, a_k, a_v, k_r, i_w * IDX_HEADS ** -0.5,
        jnp.zeros((depth, D_MODEL, LANES - IW_LANE - IDX_HEADS), F32), c_kv], axis=-1)
    assert w2.shape[-1] == W2_COLS
    wuq = w_uq.reshape(depth, Q_LORA, B_HEADS, B_QK_DIM)
    half = B_ROPE_DIM // 2
    wuq = jnp.concatenate([wuq, wuq[..., B_NOPE_DIM + half:], wuq[..., B_NOPE_DIM:B_NOPE_DIM + half]],
                          axis=-1)
    wuq = wuq.reshape(depth, Q_LORA, B_HEADS * LANES)
    wukv = w_ukv.reshape(depth, KV_LORA, B_HEADS, B_NOPE_DIM + B_V_DIM)
    wk = jnp.pad(wukv[..., :B_NOPE_DIM], ((0, 0), (0, 0), (0, 0), (0, LANES - B_NOPE_DIM)))
    wk = wk.reshape(depth, KV_LORA, B_HEADS * LANES)
    wv = wukv[..., B_NOPE_DIM:].reshape(depth, KV_LORA, B_HEADS * B_V_DIM)
    wukv = jnp.concatenate([wk, wv], axis=-1)
    return (w1.astype(BF16), w2.astype(BF16), w_g.astype(BF16), wuq.astype(BF16),
            wukv.astype(BF16))


def _rope_tables(positions):
    inv_freq = ROPE_THETA ** (-jnp.arange(0, B_ROPE_DIM, 2, dtype=F32) / B_ROPE_DIM)
    ang = positions.astype(F32).reshape(-1)[:, None] * inv_freq
    cos, sin = jnp.cos(ang), jnp.sin(ang)
    t = ang.shape[0]
    one = jnp.ones((t, B_NOPE_DIM), F32)
    z = lambda n: jnp.zeros((t, n), F32)
    cs = jnp.concatenate([one, cos, cos, z(LANES - B_QK_DIM)], axis=1)
    sn = jnp.concatenate([z(B_NOPE_DIM), -sin, sin, z(LANES - B_QK_DIM)], axis=1)
    return jnp.stack([cs, sn])


def _rel_bucket(dist):
    max_exact = REL_BUCKETS // 2
    n = jnp.maximum(dist, 0)
    nf = jnp.maximum(n.astype(F32), 1.0)
    log_b = max_exact + (jnp.log(nf / max_exact) / math.log(REL_MAX_DIST / max_exact)
                         * (REL_BUCKETS - max_exact)).astype(I32)
    return jnp.where(n < max_exact, n, jnp.minimum(log_b, REL_BUCKETS - 1))


def _bias_tables(rel_bias):
    rel = rel_bias - rel_bias[REL_BUCKETS - 1][None, :]
    q = jnp.arange(QB, dtype=I32)[:, None]
    s = jnp.arange(KC, dtype=I32)[None, :]

    def lookup(dist):
        bucket = _rel_bucket(dist)[..., None]
        out = jnp.zeros((QB, KC, A_HEADS), F32)
        for b in range(REL_BUCKETS - 1):
            out = jnp.where(bucket == b, rel[b][None, None, :], out)
        return out

    diag = jnp.where((q >= s)[..., None], lookup(q - s), 0.0)
    prev = lookup(KC + q - s)
    table = jnp.stack([prev, diag])
    table = table.reshape(2, QB, KC, NPAIR, 2)
    table = jnp.transpose(table, (0, 4, 3, 1, 2))
    return (table.reshape(2, 2, NPAIR * QB, KC) * LOG2E).astype(F32)


def kernel(x, positions, attn_norm_g, w_in, b_gate, q_latent_norm_g, kv_latent_norm_g, w_uq,
           w_ukv, w_branch_a, w_branch_b, w_out, mlp_norm_g, w_ff1, w_ff2, rel_bias,
           final_norm_g):
    bsz, seq, _ = x.shape
    depth = w_in.shape[0]
    t = bsz * seq
    k_top = min(TOPK_MAX, seq // 4)
    tm = min(512, seq)
    tq = min(512, seq)

    w1, w2, wg, wuq, wukv = _prep_weights(w_in, w_uq, w_ukv)
    wa, wb, wo = w_branch_a.astype(BF16), w_branch_b.astype(BF16), w_out.astype(BF16)
    wf1, wf2 = w_ff1.astype(BF16), w_ff2.astype(BF16)
    rope = _rope_tables(positions)
    bias = _bias_tables(rel_bias)
    gf = final_norm_g.reshape(1, D_MODEL)

    x2d = x.reshape(t, D_MODEL)
    for l in range(depth):
        g_attn = attn_norm_g[l].reshape(1, D_MODEL)
        aq, iq, kvi, iw, qm, km, vm = _proj(
            x2d, g_attn, w1[l], w2[l], q_latent_norm_g[l].reshape(1, Q_LORA),
            kv_latent_norm_g[l].reshape(1, KV_LORA), wuq[l], wukv[l], rope, bsz, seq, tm)
        ya = _dsa(aq, iq, iw, kvi, bias, bsz, seq, k_top)
        yb = _mla(qm, km, vm, tq).reshape(t, B_WIDTH)
        x2d = _merge(x2d, g_attn, wg[l], b_gate[l].reshape(1, N_BRANCHES * D_MODEL), ya, yb,
                     wa[l], wb[l], wo[l], tm)
        x2d = _mlp(x2d, mlp_norm_g[l].reshape(1, D_MODEL), wf1[l], wf2[l], gf, tm,
                   final=(l == depth - 1))
    return x2d.reshape(bsz, seq, D_MODEL)
```
